```python
import numpy as np
import jax, jax.numpy as jnp
from jax import lax

D_MODEL = 1024
BATCH = 4
SEQ = 8192
DEPTH = 1

ATT_HEADS = 8
ATT_HEAD_DIM = 64
ATT_KV_GROUPS = 2
ATT_HPG = ATT_HEADS // ATT_KV_GROUPS
ATT_WIDTH = ATT_HEADS * ATT_HEAD_DIM
KV_WIDTH = ATT_KV_GROUPS * ATT_HEAD_DIM
N_BRANCH = 3
L_CMP = 32
STRIDE_CMP = 16
CMP_HIDDEN = 256
L_SEL = 64
N_SELECT = 16
N_FORCED_LOCAL = 2
FORCE_BONUS = 1000.0
WINDOW = 512
Q_BLOCK = 128
ROPE_THETA = 500000.0
ROPE_DIM = ATT_HEAD_DIM // 4

HG_HEADS = 4
HG_DK = 128
HG_DV = 128
HG_WIDTH = HG_HEADS * HG_DV
HG_CHUNK = 64

MIX_WIDTH = ATT_WIDTH + HG_WIDTH
IN_SIZES = (ATT_WIDTH, KV_WIDTH, KV_WIDTH, KV_WIDTH, KV_WIDTH, KV_WIDTH, KV_WIDTH,
            ATT_HEADS * N_BRANCH, HG_HEADS * HG_DK, HG_HEADS * HG_DK, HG_WIDTH, HG_WIDTH)
IN_WIDTH = sum(IN_SIZES)

N_EXPERTS = 32
TOP_K = 4
D_FF = 1024
SWIGLU_LIMIT = 7.0
SWIGLU_ALPHA = 1.702
MOE_BLOCK = 256

DN_ALPHA = (2 * DEPTH) ** 0.25
DN_BETA = (8 * DEPTH) ** -0.25
LN_EPS = 1e-5
RMS_EPS = 1e-6

kernel_name = "hymba_nsa_hgrn2_moe_deepnorm"

F32 = jnp.float32


def layer_norm(x, g, b):
    xf = x.astype(F32)
    mu = jnp.mean(xf, axis=-1, keepdims=True)
    var = jnp.mean(jnp.square(xf - mu), axis=-1, keepdims=True)
    return ((xf - mu) * lax.rsqrt(var + LN_EPS) * g.astype(F32) + b.astype(F32)).astype(x.dtype)


def rope_partial(x, pos):
    half = ROPE_DIM // 2
    inv = ROPE_THETA ** (-jnp.arange(0, ROPE_DIM, 2, dtype=F32) / ROPE_DIM)
    ang = pos.astype(F32)[..., None] * inv
    cos = jnp.cos(ang)[:, :, None, :]
    sin = jnp.sin(ang)[:, :, None, :]
    xr = x[..., :ROPE_DIM].astype(F32)
    x1, x2 = xr[..., :half], xr[..., half:]
    rot = jnp.concatenate([x1 * cos - x2 * sin, x2 * cos + x1 * sin], axis=-1).astype(x.dtype)
    return jnp.concatenate([rot, x[..., ROPE_DIM:]], axis=-1)


def masked_softmax(s, mask):
    s = jnp.where(mask, s, -1e30)
    m = jnp.max(s, axis=-1, keepdims=True)
    p = jnp.exp(s - m) * mask
    return p / jnp.maximum(jnp.sum(p, axis=-1, keepdims=True), 1e-20)


def nsa_attention(q, k_cmp, v_cmp, k_sel, v_sel, k_win, v_win, gate_logits, pos,
                  pe_cmp, w_ck1, w_ck2, w_cv1, w_cv2):
    B, S = q.shape[0], q.shape[1]
    G, dh = ATT_KV_GROUPS, ATT_HEAD_DIM
    scale = dh ** -0.5
    q = rope_partial(q.reshape(B, S, ATT_HEADS, dh), pos).reshape(B, S, G, ATT_HPG, dh)
    k_cmp, v_cmp, k_sel, v_sel, k_win, v_win = [t.reshape(B, S, G, dh) for t in
                                                (k_cmp, v_cmp, k_sel, v_sel, k_win, v_win)]
    k_sel = rope_partial(k_sel, pos)
    k_win = rope_partial(k_win, pos)

    n_cmp = (S - L_CMP) // STRIDE_CMP + 1
    blk_idx = np.arange(n_cmp)[:, None] * STRIDE_CMP + np.arange(L_CMP)[None, :]
    cmp_end = blk_idx[:, -1]

    def compress(t, w1, w2):
        tb = t[:, blk_idx] + pe_cmp[None, None, :, None, :]
        tb = jnp.moveaxis(tb, 3, 2).reshape(B, n_cmp, G, L_CMP * dh)
        return jax.nn.gelu(tb @ w1) @ w2

    kc = rope_partial(compress(k_cmp, w_ck1, w_ck2), pos[:, cmp_end])
    vc = compress(v_cmp, w_cv1, w_cv2)

    n_blk = S // L_SEL
    n_pick = min(N_SELECT, n_blk)
    kbl = k_sel.reshape(B, n_blk, L_SEL, G, dh).transpose(0, 3, 1, 2, 4)
    vbl = v_sel.reshape(B, n_blk, L_SEL, G, dh).transpose(0, 3, 1, 2, 4)
    cs = np.arange(n_cmp) * STRIDE_CMP
    ss = np.arange(n_blk) * L_SEL
    overlap = jnp.asarray(((cs[:, None] < ss[None, :] + L_SEL) &
                           (ss[None, :] < cs[:, None] + L_CMP)).astype(np.float32))
    gather = jax.vmap(jax.vmap(lambda tb, ix: tb[ix]))

    kwp = jnp.pad(k_win, ((0, 0), (WINDOW, 0), (0, 0), (0, 0)))
    vwp = jnp.pad(v_win, ((0, 0), (WINDOW, 0), (0, 0), (0, 0)))
    gates = jax.nn.sigmoid(gate_logits.astype(F32)).reshape(B, S, G, ATT_HPG, N_BRANCH)
    blk_ids = jnp.arange(n_blk)

    def block(qb):
        s0 = qb * Q_BLOCK
        qblk = lax.dynamic_slice_in_dim(q, s0, Q_BLOCK, axis=1)
        t_pos = s0 + jnp.arange(Q_BLOCK)

        s = jnp.einsum('btgnd,bcgd->bgntc', qblk, kc, preferred_element_type=F32) * scale
        p_c = masked_softmax(s, cmp_end[None, :] <= t_pos[:, None])
        o_c = jnp.einsum('bgntc,bcgd->btgnd', p_c.astype(vc.dtype), vc)

        imp = jnp.einsum('bgntc,cj->bgtj', p_c, overlap)
        cur = t_pos // L_SEL
        dist = cur[:, None] - blk_ids[None, :]
        causal_blk = dist >= 0
        forced = (blk_ids[None, :] == 0) | (causal_blk & (dist < N_FORCED_LOCAL))
        imp = jnp.where(causal_blk, imp + FORCE_BONUS * forced, -1.0)
        _, sel = lax.top_k(imp, n_pick)
        ks = gather(kbl, sel).reshape(B, G, Q_BLOCK, n_pick * L_SEL, dh)
        vs = gather(vbl, sel).reshape(B, G, Q_BLOCK, n_pick * L_SEL, dh)
        kpos = (sel[..., None] * L_SEL + jnp.arange(L_SEL)).reshape(B, G, Q_BLOCK, n_pick * L_SEL)
        s = jnp.einsum('btgnd,bgtkd->bgntk', qblk, ks, preferred_element_type=F32) * scale
        p_s = masked_softmax(s, (kpos <= t_pos[None, None, :, None])[:, :, None])
        o_s = jnp.einsum('bgntk,bgtkd->btgnd', p_s.astype(vs.dtype), vs)

        kw = lax.dynamic_slice_in_dim(kwp, s0, WINDOW + Q_BLOCK, axis=1)
        vw = lax.dynamic_slice_in_dim(vwp, s0, WINDOW + Q_BLOCK, axis=1)
        kp = s0 - WINDOW + jnp.arange(WINDOW + Q_BLOCK)
        m_w = ((kp[None, :] <= t_pos[:, None]) & (kp[None, :] > t_pos[:, None] - WINDOW)
               & (kp[None, :] >= 0))
        s = jnp.einsum('btgnd,bkgd->bgntk', qblk, kw, preferred_element_type=F32) * scale
        p_w = masked_softmax(s, m_w)
        o_w = jnp.einsum('bgntk,bkgd->btgnd', p_w.astype(vw.dtype), vw)

        gb = lax.dynamic_slice_in_dim(gates, s0, Q_BLOCK, axis=1)
        o = gb[..., 0:1] * o_c + gb[..., 1:2] * o_s + gb[..., 2:3] * o_w
        return o.astype(q.dtype)

    out = lax.map(block, jnp.arange(S // Q_BLOCK))
    return jnp.moveaxis(out, 0, 1).reshape(B, S, ATT_WIDTH)


def hgrn2(q, zf, inp, g, lb, norm_g):
    B, S = q.shape[0], q.shape[1]
    H, dk, dv, C = HG_HEADS, HG_DK, HG_DV, HG_CHUNK
    f_hat = jax.nn.sigmoid(zf.astype(F32))
    logf = jnp.log(lb + (1.0 - lb) * f_hat)
    k = (1.0 - lb) * (1.0 - f_hat)

    def heads(t, d):
        return t.reshape(B, S // C, C, H, d).transpose(1, 0, 3, 2, 4)

    qs, ks, ls, vs = heads(q.astype(F32), dk), heads(k, dk), heads(logf, dk), heads(inp.astype(F32), dv)
    causal = jnp.asarray(np.tril(np.ones((C, C), dtype=bool)))[:, :, None]

    def step(state, xs):
        qc, kc, lc, vc = xs
        b = jnp.cumsum(lc, axis=2)
        o_inter = jnp.einsum('bhtk,bhkv->bhtv', qc * jnp.exp(b), state)
        diff = b[:, :, :, None, :] - b[:, :, None, :, :]
        decay = jnp.where(causal, jnp.exp(jnp.where(causal, diff, 0.0)), 0.0)
        A = jnp.einsum('bhtk,bhsk,bhtsk->bhts', qc, kc, decay)
        o = o_inter + jnp.einsum('bhts,bhsv->bhtv', A, vc)
        b_last = b[:, :, -1]
        state = (jnp.exp(b_last)[..., None] * state +
                 jnp.einsum('bhsk,bhsv->bhkv', kc * jnp.exp(b_last[:, :, None] - b), vc))
        return state, o

    state0 = jnp.zeros((B, H, dk, dv), F32)
    _, o = lax.scan(step, state0, (qs, ks, ls, vs))
    o = o.transpose(1, 0, 3, 2, 4).reshape(B, S, H, dv)
    o = o * lax.rsqrt(jnp.mean(jnp.square(o), axis=-1, keepdims=True) + RMS_EPS) * norm_g.astype(F32)
    o = o.reshape(B, S, HG_WIDTH) * jax.nn.silu(g.astype(F32))
    return o.astype(q.dtype)


def moe(x, w_r, b_r, w_gu, b_gu, w_dn, b_dn):
    B, S, D = x.shape
    N = B * S
    A = N * TOP_K
    xf = x.reshape(N, D)
    logits = (xf @ w_r + b_r).astype(F32)
    top_v, top_i = lax.top_k(logits, TOP_K)
    top_w = jax.nn.softmax(top_v, axis=-1)
    e_flat = top_i.reshape(A)
    tok_flat = jnp.arange(A) // TOP_K
    w_flat = top_w.reshape(A)
    order = jnp.argsort(e_flat)
    e_sorted, tok_sorted, w_sorted = e_flat[order], tok_flat[order], w_flat[order]
    counts = jnp.bincount(e_flat, length=N_EXPERTS)
    padded = (counts + MOE_BLOCK - 1) // MOE_BLOCK * MOE_BLOCK
    starts = jnp.cumsum(counts) - counts
    pends = jnp.cumsum(padded)
    pstarts = pends - padded
    dest = pstarts[e_sorted] + jnp.arange(A) - starts[e_sorted]
    n_blk = -(-(A + N_EXPERTS * (MOE_BLOCK - 1)) // MOE_BLOCK)
    P = n_blk * MOE_BLOCK
    row_tok = jnp.zeros((P,), jnp.int32).at[dest].set(tok_sorted)
    row_w = jnp.zeros((P,), F32).at[dest].set(w_sorted)
    blk_expert = jnp.clip(jnp.searchsorted(pends, jnp.arange(n_blk) * MOE_BLOCK, side='right'),
                          0, N_EXPERTS - 1)
    xs = xf[row_tok].reshape(n_blk, MOE_BLOCK, D)

    def expert_block(args):
        xb, e = args
        h = xb @ w_gu[e] + b_gu[e]
        gate = jnp.minimum(h[:, :D_FF], SWIGLU_LIMIT)
        up = jnp.clip(h[:, D_FF:], -SWIGLU_LIMIT, SWIGLU_LIMIT)
        act = gate * jax.nn.sigmoid(SWIGLU_ALPHA * gate) * (up + 1.0)
        return act @ w_dn[e] + b_dn[e]

    ys = lax.map(expert_block, (xs, blk_expert)).reshape(P, D)
    out = jnp.zeros((N, D), F32).at[row_tok].add(ys.astype(F32) * row_w[:, None])
    return out.astype(x.dtype).reshape(B, S, D)


def setup_inputs(seed: int = 0) -> dict:
    key = jax.random.key(seed)
    ks = jax.random.split(key, 24)
    nrm = lambda k, shape, s: jax.random.normal(k, shape, F32) * s
    L, D, dh = DEPTH, D_MODEL, ATT_HEAD_DIM
    return {
        "x": nrm(ks[0], (BATCH, SEQ, D), 1.0),
        "positions": jnp.broadcast_to(jnp.arange(SEQ, dtype=jnp.int32), (BATCH, SEQ)),
        "w_in": nrm(ks[1], (L, D, IN_WIDTH), D ** -0.5),
        "pe_cmp": nrm(ks[2], (L, L_CMP, dh), 0.1),
        "w_ck1": nrm(ks[3], (L, L_CMP * dh, CMP_HIDDEN), (L_CMP * dh) ** -0.5),
        "w_ck2": nrm(ks[4], (L, CMP_HIDDEN, dh), CMP_HIDDEN ** -0.5),
        "w_cv1": nrm(ks[5], (L, L_CMP * dh, CMP_HIDDEN), (L_CMP * dh) ** -0.5),
        "w_cv2": nrm(ks[6], (L, CMP_HIDDEN, dh), CMP_HIDDEN ** -0.5),
        "hg_lb": nrm(ks[7], (L + 1, HG_HEADS * HG_DK), 0.1),
        "hg_norm_g": 1.0 + nrm(ks[8], (L, HG_DV), 0.01),
        "w_o": nrm(ks[9], (L, MIX_WIDTH, D), MIX_WIDTH ** -0.5 * DN_BETA),
        "ln1_g": 1.0 + nrm(ks[10], (L, D), 0.01),
        "ln1_b": nrm(ks[11], (L, D), 0.01),
        "w_router": nrm(ks[12], (L, D, N_EXPERTS), D ** -0.5),
        "b_router": nrm(ks[13], (L, N_EXPERTS), 0.01),
        "w_gate_up": nrm(ks[14], (L, N_EXPERTS, D, 2 * D_FF), D ** -0.5),
        "b_gate_up": nrm(ks[15], (L, N_EXPERTS, 2 * D_FF), 0.01),
        "w_down": nrm(ks[16], (L, N_EXPERTS, D_FF, D), D_FF ** -0.5 * DN_BETA),
        "b_down": nrm(ks[17], (L, N_EXPERTS, D), 0.01),
        "ln2_g": 1.0 + nrm(ks[18], (L, D), 0.01),
        "ln2_b": nrm(ks[19], (L, D), 0.01),
    }


def reference(x, positions, w_in, pe_cmp, w_ck1, w_ck2, w_cv1, w_cv2, hg_lb, hg_norm_g,
              w_o, ln1_g, ln1_b, w_router, b_router, w_gate_up, b_gate_up, w_down, b_down,
              ln2_g, ln2_b):
    lbs = jnp.cumsum(jax.nn.softmax(hg_lb.astype(F32), axis=0), axis=0)
    split_at = np.cumsum(IN_SIZES)[:-1].tolist()
    h = x
    for l in range(DEPTH):
        proj = h @ w_in[l]
        (q_a, k_c, v_c, k_s, v_s, k_w, v_w, g_a, q_h, f_h, i_h, g_h) = jnp.split(proj, split_at, axis=-1)
        att = nsa_attention(q_a, k_c, v_c, k_s, v_s, k_w, v_w, g_a, positions,
                            pe_cmp[l], w_ck1[l], w_ck2[l], w_cv1[l], w_cv2[l])
        rec = hgrn2(q_h, f_h, i_h, g_h, lbs[l], hg_norm_g[l])
        mix = jnp.concatenate([att, rec], axis=-1) @ w_o[l]
        h = layer_norm(DN_ALPHA * h + mix, ln1_g[l], ln1_b[l])
        ffn = moe(h, w_router[l], b_router[l], w_gate_up[l], b_gate_up[l], w_down[l], b_down[l])
        h = layer_norm(DN_ALPHA * h + ffn, ln2_g[l], ln2_b[l])
    return h
```

```python
import functools

import numpy as np
import jax
import jax.numpy as jnp
from jax import lax
from jax.experimental import pallas as pl
from jax.experimental.pallas import tpu as pltpu

F32 = jnp.float32
BF16 = jnp.bfloat16

D_MODEL = 1024
ATT_HEADS = 8
ATT_HEAD_DIM = 64
ATT_KV_GROUPS = 2
ATT_HPG = ATT_HEADS // ATT_KV_GROUPS
ATT_WIDTH = ATT_HEADS * ATT_HEAD_DIM
KV_WIDTH = ATT_KV_GROUPS * ATT_HEAD_DIM
N_BRANCH = 3
L_CMP = 32
STRIDE_CMP = 16
CMP_HIDDEN = 256
L_SEL = 64
N_SELECT = 16
N_FORCED_LOCAL = 2
FORCE_BONUS = 1000.0
WINDOW = 512
Q_BLOCK = 128
ROPE_THETA = 500000.0
ROPE_DIM = ATT_HEAD_DIM // 4
ROPE_HALF = ROPE_DIM // 2
HG_HEADS = 4
HG_DK = 128
HG_DV = 128
HG_WIDTH = HG_HEADS * HG_DV
HG_CHUNK = 64
HG_SUB = 16
N_EXPERTS = 32
TOP_K = 4
D_FF = 1024
SWIGLU_LIMIT = 7.0
SWIGLU_ALPHA = 1.702
DEPTH = 1
DN_ALPHA = (2 * DEPTH) ** 0.25
LN_EPS = 1e-5
RMS_EPS = 1e-6
NEG = -1e30
SEL_SHIFT = L_SEL.bit_length() - 1
SUB_SHIFT = HG_SUB.bit_length() - 1

LANES = 128
VMEM_LIMIT = 56 * 1024 * 1024
IN_TM = 512
SEL_TK = 512
HG_TC = 512
MOE_BM = 256
CMB_TM = 256

C_Q = 0
C_KCMP = 512
C_VCMP = 640
C_KVA = 768
C_GATE = 1280
C_HG = 1408
IN_COLS = C_HG + 4 * HG_WIDTH
N_GATE = ATT_HEADS * N_BRANCH


def _nt(a, b):
    return lax.dot_general(a, b, (((1,), (1,)), ((), ())), preferred_element_type=F32)


def _tn(a, b):
    return lax.dot_general(a, b, (((0,), (0,)), ((), ())), preferred_element_type=F32)


def _mm(a, b):
    return jnp.dot(a, b, preferred_element_type=F32)


def _split3(x):
    hi = x.astype(BF16)
    r1 = x - hi.astype(F32)
    mid = r1.astype(BF16)
    lo = (r1 - mid.astype(F32)).astype(BF16)
    return hi, mid, lo


def _rope(v, c, s1, s2):
    return v * c + pltpu.roll(v, LANES - ROPE_HALF, 1) * s1 + pltpu.roll(v, ROPE_HALF, 1) * s2


def _cparams(sem):
    return pltpu.CompilerParams(dimension_semantics=sem, vmem_limit_bytes=VMEM_LIMIT)


def _in_proj_kernel(x_ref, w_ref, c_ref, s1_ref, s2_ref,
                    qx_ref, kcmp_ref, vcmp_ref, kva_ref, gate_ref, hg_ref):
    xb = x_ref[...].astype(BF16)
    c, s1, s2 = c_ref[...], s1_ref[...], s2_ref[...]
    lane = lax.broadcasted_iota(jnp.int32, (1, LANES), 1)
    low = lane < ATT_HEAD_DIM

    q = _mm(xb, w_ref[:, C_Q:C_Q + ATT_WIDTH])
    scale = ATT_HEAD_DIM ** -0.5
    for j in range(ATT_WIDTH // LANES):
        v = _rope(q[:, j * LANES:(j + 1) * LANES], c, s1, s2) * scale
        vr = pltpu.roll(v, ATT_HEAD_DIM, 1)
        g = (2 * j) // ATT_HPG
        keep = low if g == 0 else jnp.logical_not(low)
        h0 = jnp.where(keep, v if g == 0 else vr, 0.0)
        h1 = jnp.where(keep, vr if g == 0 else v, 0.0)
        qx_ref[:, (2 * j) * LANES:(2 * j + 1) * LANES] = h0.astype(BF16)
        qx_ref[:, (2 * j + 1) * LANES:(2 * j + 2) * LANES] = h1.astype(BF16)

    kv = _mm(xb, w_ref[:, C_KCMP:C_GATE])
    kcmp_ref[...] = kv[:, 0:128]
    vcmp_ref[...] = kv[:, 128:256]
    kva_ref[:, 0:128] = _rope(kv[:, 256:384], c, s1, s2).astype(BF16)
    kva_ref[:, 128:256] = kv[:, 384:512].astype(BF16)
    kva_ref[:, 256:384] = _rope(kv[:, 512:640], c, s1, s2).astype(BF16)
    kva_ref[:, 384:512] = kv[:, 640:768].astype(BF16)

    gate_ref[...] = jax.nn.sigmoid(_mm(xb, w_ref[:, C_GATE:C_HG]))
    for j in range(4):
        hg_ref[:, j * HG_WIDTH:(j + 1) * HG_WIDTH] = _mm(
            xb, w_ref[:, C_HG + j * HG_WIDTH:C_HG + (j + 1) * HG_WIDTH])


def _in_proj(x2, w_pad, c_tab, s1_tab, s2_tab):
    n = x2.shape[0]
    tm = IN_TM
    row = lambda w: pl.BlockSpec((tm, w), lambda i: (i, 0))
    return pl.pallas_call(
        _in_proj_kernel,
        grid=(n // tm,),
        in_specs=[row(D_MODEL), pl.BlockSpec((D_MODEL, IN_COLS), lambda i: (0, 0)),
                  row(LANES), row(LANES), row(LANES)],
        out_specs=[row(ATT_HEADS * LANES), row(LANES), row(LANES), row(4 * LANES), row(LANES),
                   row(4 * HG_WIDTH)],
        out_shape=[jax.ShapeDtypeStruct((n, ATT_HEADS * LANES), BF16),
                   jax.ShapeDtypeStruct((n, LANES), F32),
                   jax.ShapeDtypeStruct((n, LANES), F32),
                   jax.ShapeDtypeStruct((n, 4 * LANES), BF16),
                   jax.ShapeDtypeStruct((n, LANES), F32),
                   jax.ShapeDtypeStruct((n, 4 * HG_WIDTH), F32)],
        compiler_params=_cparams(("arbitrary",)),
        name="in_proj",
    )(x2, w_pad, c_tab, s1_tab, s2_tab)


def _compress_kernel(kf_ref, vf_ref, pet_ref, peb_ref, wkt_ref, wkb_ref, wvt_ref, wvb_ref,
                     wk2_ref, wv2_ref, c_ref, s1_ref, s2_ref, kc_ref, vc_ref):
    nc = kf_ref.shape[0]

    def mlp(x, wt, wb, w2):
        top = (x + pet_ref[...]).astype(BF16)
        bot = (x + peb_ref[...]).astype(BF16)
        u = _mm(top, wt[...])
        v = _mm(bot, wb[...])
        h = u + pltpu.roll(v, nc - 1, 0)
        return _mm(jax.nn.gelu(h).astype(BF16), w2[...])

    kc = mlp(kf_ref[...], wkt_ref, wkb_ref, wk2_ref)
    kc_ref[...] = _rope(kc, c_ref[...], s1_ref[...], s2_ref[...]).astype(BF16)
    vc_ref[...] = mlp(vf_ref[...], wvt_ref, wvb_ref, wv2_ref).astype(BF16)


def _compress(kf, vf, pet, peb, wkt, wkb, wvt, wvb, wk2, wv2, cc, s1c, s2c):
    b, nc, w = kf.shape
    per_b = lambda width: pl.BlockSpec((None, nc, width), lambda i: (i, 0, 0))
    full = lambda a: pl.BlockSpec(a.shape, lambda i: (0,) * a.ndim)
    return pl.pallas_call(
        _compress_kernel,
        grid=(b,),
        in_specs=[per_b(w), per_b(w), full(pet), full(peb), full(wkt), full(wkb), full(wvt),
                  full(wvb), full(wk2), full(wv2), per_b(LANES), per_b(LANES), per_b(LANES)],
        out_specs=[per_b(LANES), per_b(LANES)],
        out_shape=[jax.ShapeDtypeStruct((b, nc, LANES), BF16)] * 2,
        compiler_params=_cparams(("arbitrary",)),
        name="compress",
    )(kf, vf, pet, peb, wkt, wkb, wvt, wvb, wk2, wv2, cc, s1c, s2c)


def _masked_softmax_rows(s, mask):
    s = jnp.where(mask, s, NEG)
    m = jnp.max(s, axis=-1, keepdims=True)
    p = jnp.where(mask, jnp.exp(s - m), 0.0)
    return p / jnp.maximum(jnp.sum(p, axis=-1, keepdims=True), 1e-20)


def _nsa_kernel(qx_ref, gate_ref, kc_ref, vc_ref, kva_ref, ovl_ref, out_ref,
                m_ref, l_ref, acc_ref, *, seq):
    t = Q_BLOCK
    rows = ATT_HPG * t
    qb = pl.program_id(1)
    s0 = qb * t
    t_pos = s0 + lax.broadcasted_iota(jnp.int32, (t, 1), 0)
    nc = kc_ref.shape[0]
    gates = gate_ref[...]

    def tile4(a):
        return jnp.concatenate([a] * ATT_HPG, axis=0)

    pieces = []
    for g in range(ATT_KV_GROUPS):
        qg = jnp.concatenate(
            [qx_ref[:, (g * ATT_HPG + n) * LANES:(g * ATT_HPG + n + 1) * LANES]
             for n in range(ATT_HPG)], axis=0)

        cmp_end = lax.broadcasted_iota(jnp.int32, (1, nc), 1) * STRIDE_CMP + (L_CMP - 1)
        mask_c = tile4((cmp_end <= t_pos) & (cmp_end < seq))
        p_c = _masked_softmax_rows(_nt(qg, kc_ref[...]), mask_c).astype(BF16)
        o_c = _mm(p_c, vc_ref[...])
        imp4 = _mm(p_c, ovl_ref[...])
        imp = imp4[0:t]
        for n in range(1, ATT_HPG):
            imp = imp + imp4[n * t:(n + 1) * t]

        blk = lax.broadcasted_iota(jnp.int32, (1, LANES), 1)
        dist = jnp.right_shift(t_pos, SEL_SHIFT) - blk
        causal_blk = dist >= 0
        forced = (blk == 0) | (causal_blk & (dist < N_FORCED_LOCAL))
        imp = jnp.where(causal_blk, imp + jnp.where(forced, FORCE_BONUS, 0.0), -1.0)
        imp_t = imp.T
        brow = lax.broadcasted_iota(jnp.int32, (LANES, t), 0)
        sel_t = jnp.zeros((LANES, t), F32)
        for _ in range(N_SELECT):
            mx = jnp.max(imp_t, axis=0, keepdims=True)
            idx = jnp.min(jnp.where(imp_t == mx, brow, LANES), axis=0, keepdims=True)
            pick = brow == idx
            sel_t = jnp.where(pick, 1.0, sel_t)
            imp_t = jnp.where(pick, -3e38, imp_t)
        sel = sel_t.T.astype(BF16)

        m_ref[...] = jnp.full((rows, 1), NEG, F32)
        l_ref[...] = jnp.zeros((rows, 1), F32)
        acc_ref[...] = jnp.zeros((rows, LANES), F32)

        def sel_tile(kt, carry):
            k0 = pl.multiple_of(kt * SEL_TK, SEL_TK)
            k_t = kva_ref[pl.ds(k0, SEL_TK), 0:128]
            v_t = kva_ref[pl.ds(k0, SEL_TK), 128:256]
            kpos = k0 + lax.broadcasted_iota(jnp.int32, (1, SEL_TK), 1)
            erow = lax.broadcasted_iota(jnp.int32, (LANES, 1), 0)
            expand = jnp.where(erow == jnp.right_shift(kpos, SEL_SHIFT), 1.0, 0.0).astype(BF16)
            ok = (_mm(sel, expand) > 0.5) & (kpos <= t_pos)
            s = _nt(qg, k_t) + tile4(jnp.where(ok, 0.0, NEG))
            m_old = m_ref[...]
            m_new = jnp.maximum(m_old, jnp.max(s, axis=-1, keepdims=True))
            alpha = jnp.exp(m_old - m_new)
            p = jnp.exp(s - m_new)
            l_ref[...] = alpha * l_ref[...] + jnp.sum(p, axis=-1, keepdims=True)
            acc_ref[...] = alpha * acc_ref[...] + _mm(p.astype(BF16), v_t)
            m_ref[...] = m_new
            return carry

        n_tiles = (s0 + t + SEL_TK - 1) // SEL_TK
        lax.fori_loop(0, n_tiles, sel_tile, 0)
        o_s = acc_ref[...] / jnp.maximum(l_ref[...], 1e-20)

        wlen = WINDOW + t
        ws = pl.multiple_of(jnp.maximum(s0 - WINDOW, 0), t)
        k_w = kva_ref[pl.ds(ws, wlen), 256:384]
        v_w = kva_ref[pl.ds(ws, wlen), 384:512]
        kp = ws + lax.broadcasted_iota(jnp.int32, (1, wlen), 1)
        mask_w = tile4((kp <= t_pos) & (kp > t_pos - WINDOW))
        p_w = _masked_softmax_rows(_nt(qg, k_w), mask_w).astype(BF16)
        o_w = _mm(p_w, v_w)

        for n in range(ATT_HPG):
            c0 = (g * ATT_HPG + n) * N_BRANCH
            r = slice(n * t, (n + 1) * t)
            ln = slice(g * ATT_HEAD_DIM, (g + 1) * ATT_HEAD_DIM)
            pieces.append(gates[:, c0:c0 + 1] * o_c[r, ln]
                          + gates[:, c0 + 1:c0 + 2] * o_s[r, ln]
                          + gates[:, c0 + 2:c0 + 3] * o_w[r, ln])
    out_ref[...] = jnp.concatenate(pieces, axis=1).astype(BF16)


def _nsa(qx, gate, kc, vc, kva, ovl, batch, seq):
    n = qx.shape[0]
    nqb = seq // Q_BLOCK
    nc = kc.shape[1]
    row = lambda w: pl.BlockSpec((Q_BLOCK, w), lambda b, i: (b * nqb + i, 0))
    return pl.pallas_call(
        functools.partial(_nsa_kernel, seq=seq),
        grid=(batch, nqb),
        in_specs=[row(ATT_HEADS * LANES), row(LANES),
                  pl.BlockSpec((None, nc, LANES), lambda b, i: (b, 0, 0)),
                  pl.BlockSpec((None, nc, LANES), lambda b, i: (b, 0, 0)),
                  pl.BlockSpec((None, seq, 4 * LANES), lambda b, i: (b, 0, 0)),
                  pl.BlockSpec(ovl.shape, lambda b, i: (0, 0))],
        out_specs=row(ATT_WIDTH),
        out_shape=jax.ShapeDtypeStruct((n, ATT_WIDTH), BF16),
        scratch_shapes=[pltpu.VMEM((ATT_HPG * Q_BLOCK, 1), F32),
                        pltpu.VMEM((ATT_HPG * Q_BLOCK, 1), F32),
                        pltpu.VMEM((ATT_HPG * Q_BLOCK, LANES), F32)],
        compiler_params=_cparams(("arbitrary", "arbitrary")),
        name="nsa",
    )(qx, gate, kc, vc, kva.reshape(batch, seq, 4 * LANES), ovl)


def _hgrn2_kernel(q_ref, f_ref, i_ref, g_ref, lb_ref, ng_ref, out_ref, st_ref, logf_ref, kk_ref):
    c = HG_CHUNK
    sub = HG_SUB
    nsub = c // sub

    @pl.when(pl.program_id(2) == 0)
    def _():
        st_ref[...] = jnp.zeros_like(st_ref)

    lb = lb_ref[...]
    f_hat = jax.nn.sigmoid(f_ref[...])
    logf_ref[...] = jnp.log(lb + (1.0 - lb) * f_hat)
    kk_ref[...] = (1.0 - lb) * (1.0 - f_hat)

    ri = lax.broadcasted_iota(jnp.int32, (c, c), 0)
    ci = lax.broadcasted_iota(jnp.int32, (c, c), 1)
    tril = jnp.where(ci <= ri, 1.0, 0.0).astype(BF16)
    row = lax.broadcasted_iota(jnp.int32, (c, 1), 0)
    row_sub = jnp.right_shift(row, SUB_SHIFT)
    row_loc = row - row_sub * sub
    ng = ng_ref[...]

    def chunk(ic, carry):
        r0 = pl.multiple_of(ic * c, c)
        q = q_ref[pl.ds(r0, c), :]
        k = kk_ref[pl.ds(r0, c), :]
        v = i_ref[pl.ds(r0, c), :].astype(BF16)
        lf_hi, lf_mid, lf_lo = _split3(logf_ref[pl.ds(r0, c), :])
        b = _mm(tril, lf_hi) + _mm(tril, lf_mid) + _mm(tril, lf_lo)
        st = st_ref[...]
        o = _nt((q * jnp.exp(b)).astype(BF16), st.astype(BF16))

        a_rows = [jnp.zeros((sub, c), F32)]
        for i in range(1, nsub):
            ref_b = b[i * sub:i * sub + 1, :]
            qs = q[i * sub:(i + 1) * sub, :] * jnp.exp(b[i * sub:(i + 1) * sub, :] - ref_b)
            ks = jnp.where(row < i * sub, k * jnp.exp(jnp.minimum(ref_b - b, 0.0)), 0.0)
            a_rows.append(_nt(qs.astype(BF16), ks.astype(BF16)))
        a = jnp.concatenate(a_rows, axis=0)

        b3 = b.reshape(nsub, sub, HG_DK)
        q3 = q.reshape(nsub, sub, HG_DK)
        k3 = k.reshape(nsub, sub, HG_DK)
        for s in range(sub):
            dec = jnp.exp(jnp.minimum(b3 - b3[:, s:s + 1, :], 0.0))
            w = (q3 * k3[:, s:s + 1, :] * dec).reshape(c, HG_DK)
            col = jnp.sum(w, axis=-1, keepdims=True)
            hit = (ci == row_sub * sub + s) & (row_loc >= s)
            a = a + jnp.where(hit, col, 0.0)

        o = o + _mm(a.astype(BF16), v)
        b_last = b[c - 1:c, :]
        kd = (k * jnp.exp(b_last - b)).astype(BF16)
        st_ref[...] = st * jnp.exp(b_last) + _tn(v, kd)

        o = o * lax.rsqrt(jnp.mean(o * o, axis=-1, keepdims=True) + RMS_EPS) * ng
        out_ref[pl.ds(r0, c), :] = (o * jax.nn.silu(g_ref[pl.ds(r0, c), :])).astype(BF16)
        return carry

    lax.fori_loop(0, q_ref.shape[0] // c, chunk, 0)


def _hgrn2(hg, lb, ng, batch, seq):
    n = hg.shape[0]
    tc = HG_TC
    nt = seq // tc
    col = lambda j: pl.BlockSpec((tc, HG_DK), lambda b, h, i, j=j: (b * nt + i, j * HG_HEADS + h))
    return pl.pallas_call(
        _hgrn2_kernel,
        grid=(batch, HG_HEADS, nt),
        in_specs=[col(0), col(1), col(2), col(3),
                  pl.BlockSpec((1, HG_DK), lambda b, h, i: (0, h)),
                  pl.BlockSpec((1, HG_DV), lambda b, h, i: (0, 0))],
        out_specs=pl.BlockSpec((tc, HG_DV), lambda b, h, i: (b * nt + i, h)),
        out_shape=jax.ShapeDtypeStruct((n, HG_WIDTH), BF16),
        scratch_shapes=[pltpu.VMEM((HG_DV, HG_DK), F32),
                        pltpu.VMEM((tc, HG_DK), F32),
                        pltpu.VMEM((tc, HG_DK), F32)],
        compiler_params=_cparams(("arbitrary", "arbitrary", "arbitrary")),
        name="hgrn2",
    )(hg, hg, hg, hg, lb, ng)


def _layer_norm(y, g, b):
    mu = jnp.mean(y, axis=-1, keepdims=True)
    d = y - mu
    var = jnp.mean(d * d, axis=-1, keepdims=True)
    return d * lax.rsqrt(var + LN_EPS) * g + b


def _out_proj_kernel(att_ref, rec_ref, x_ref, wo_ref, g_ref, b_ref, wr_ref, br_ref,
                     h_ref, ri_ref, rw_ref):
    mix = _mm(att_ref[...], wo_ref[0:ATT_WIDTH, :]) + _mm(rec_ref[...], wo_ref[ATT_WIDTH:, :])
    h = _layer_norm(DN_ALPHA * x_ref[...] + mix, g_ref[...], b_ref[...])
    h_ref[...] = h

    hh, hm, hl = _split3(h)
    wh, wm, wl = wr_ref[0], wr_ref[1], wr_ref[2]
    lg = (_mm(hh, wh) + _mm(hh, wm) + _mm(hm, wh) + _mm(hh, wl) + _mm(hl, wh) + _mm(hm, wm)
          + br_ref[...])
    lane = lax.broadcasted_iota(jnp.int32, lg.shape, 1)
    ri = jnp.zeros(lg.shape, jnp.int32)
    rw = jnp.zeros(lg.shape, F32)
    v0 = None
    for r in range(TOP_K):
        mx = jnp.max(lg, axis=-1, keepdims=True)
        idx = jnp.min(jnp.where(lg == mx, lane, LANES), axis=-1, keepdims=True)
        if r == 0:
            v0 = mx
        ri = jnp.where(lane == r, idx, ri)
        rw = jnp.where(lane == r, jnp.exp(mx - v0), rw)
        lg = jnp.where(lane == idx, -3e38, lg)
    ri_ref[...] = ri
    rw_ref[...] = rw / jnp.sum(rw, axis=-1, keepdims=True)


def _out_proj(att, rec, x2, wo, g, b, wr3, br):
    n = x2.shape[0]
    tm = IN_TM
    row = lambda w: pl.BlockSpec((tm, w), lambda i: (i, 0))
    full = lambda a: pl.BlockSpec(a.shape, lambda i: (0,) * a.ndim)
    return pl.pallas_call(
        _out_proj_kernel,
        grid=(n // tm,),
        in_specs=[row(ATT_WIDTH), row(HG_WIDTH), row(D_MODEL), full(wo), full(g), full(b),
                  full(wr3), full(br)],
        out_specs=[row(D_MODEL), row(LANES), row(LANES)],
        out_shape=[jax.ShapeDtypeStruct((n, D_MODEL), F32),
                   jax.ShapeDtypeStruct((n, LANES), jnp.int32),
                   jax.ShapeDtypeStruct((n, LANES), F32)],
        compiler_params=_cparams(("arbitrary",)),
        name="out_proj",
    )(att, rec, x2, wo, g, b, wr3, br)


def _experts_kernel(be_ref, tok_ref, h_hbm, wgu_ref, bgu_ref, wdn_ref, bdn_ref, y_ref,
                    xbuf, sem):
    i = pl.program_id(0)
    nblk = pl.num_programs(0)
    bm = xbuf.shape[1]

    def gather(blk, slot):
        def issue(r, carry):
            tok = tok_ref[blk * bm + r]
            pltpu.make_async_copy(h_hbm.at[pl.ds(tok, 1), :], xbuf.at[slot, pl.ds(r, 1), :],
                                  sem.at[slot]).start()
            return carry
        lax.fori_loop(0, bm, issue, 0)

    @pl.when(i == 0)
    def _():
        gather(0, 0)

    @pl.when(i + 1 < nblk)
    def _():
        gather(i + 1, (i + 1) % 2)

    slot = i % 2
    pltpu.make_async_copy(h_hbm.at[pl.ds(0, bm), :], xbuf.at[slot], sem.at[slot]).wait()

    xb = xbuf[slot].astype(BF16)
    h = _mm(xb, wgu_ref[...]) + bgu_ref[...]
    gate = jnp.minimum(h[:, :D_FF], SWIGLU_LIMIT)
    up = jnp.clip(h[:, D_FF:], -SWIGLU_LIMIT, SWIGLU_LIMIT)
    act = gate * jax.nn.sigmoid(SWIGLU_ALPHA * gate) * (up + 1.0)
    y_ref[...] = _mm(act.astype(BF16), wdn_ref[...]) + bdn_ref[...]


def _experts(blk_expert, row_tok, h1, wgu, bgu, wdn, bdn):
    nblk = blk_expert.shape[0]
    bm = MOE_BM
    grid_spec = pltpu.PrefetchScalarGridSpec(
        num_scalar_prefetch=2,
        grid=(nblk,),
        in_specs=[pl.BlockSpec(memory_space=pl.ANY),
                  pl.BlockSpec((None, D_MODEL, 2 * D_FF), lambda i, be, tok: (be[i], 0, 0)),
                  pl.BlockSpec((None, 1, 2 * D_FF), lambda i, be, tok: (be[i], 0, 0)),
                  pl.BlockSpec((None, D_FF, D_MODEL), lambda i, be, tok: (be[i], 0, 0)),
                  pl.BlockSpec((None, 1, D_MODEL), lambda i, be, tok: (be[i], 0, 0))],
        out_specs=pl.BlockSpec((bm, D_MODEL), lambda i, be, tok: (i, 0)),
        scratch_shapes=[pltpu.VMEM((2, bm, D_MODEL), F32), pltpu.SemaphoreType.DMA((2,))],
    )
    return pl.pallas_call(
        _experts_kernel,
        grid_spec=grid_spec,
        out_shape=jax.ShapeDtypeStruct((nblk * bm, D_MODEL), F32),
        compiler_params=_cparams(("arbitrary",)),
        name="experts",
    )(blk_expert, row_tok, h1, wgu, bgu, wdn, bdn)


def _combine_kernel(pos_ref, y_hbm, h_ref, rw_ref, g_ref, b_ref, out_ref, ybuf, sem):
    i = pl.program_id(0)
    nstep = pl.num_programs(0)
    tm = ybuf.shape[2]

    def gather(step, slot):
        def issue(r, carry):
            for k in range(TOP_K):
                p = pos_ref[(step * tm + r) * TOP_K + k]
                pltpu.make_async_copy(y_hbm.at[pl.ds(p, 1), :], ybuf.at[slot, k, pl.ds(r, 1), :],
                                      sem.at[slot]).start()
            return carry
        lax.fori_loop(0, tm, issue, 0)

    @pl.when(i == 0)
    def _():
        gather(0, 0)

    @pl.when(i + 1 < nstep)
    def _():
        gather(i + 1, (i + 1) % 2)

    slot = i % 2
    for k in range(TOP_K):
        pltpu.make_async_copy(y_hbm.at[pl.ds(0, tm), :], ybuf.at[slot, k], sem.at[slot]).wait()

    rw = rw_ref[...]
    ffn = rw[:, 0:1] * ybuf[slot, 0]
    for k in range(1, TOP_K):
        ffn = ffn + rw[:, k:k + 1] * ybuf[slot, k]
    out_ref[...] = _layer_norm(DN_ALPHA * h_ref[...] + ffn, g_ref[...], b_ref[...])


def _combine(pos, ys, h1, rw, g, b):
    n = h1.shape[0]
    tm = CMB_TM
    grid_spec = pltpu.PrefetchScalarGridSpec(
        num_scalar_prefetch=1,
        grid=(n // tm,),
        in_specs=[pl.BlockSpec(memory_space=pl.ANY),
                  pl.BlockSpec((tm, D_MODEL), lambda i, pos: (i, 0)),
                  pl.BlockSpec((tm, LANES), lambda i, pos: (i, 0)),
                  pl.BlockSpec((1, D_MODEL), lambda i, pos: (0, 0)),
                  pl.BlockSpec((1, D_MODEL), lambda i, pos: (0, 0))],
        out_specs=pl.BlockSpec((tm, D_MODEL), lambda i, pos: (i, 0)),
        scratch_shapes=[pltpu.VMEM((2, TOP_K, tm, D_MODEL), F32), pltpu.SemaphoreType.DMA((2,))],
    )
    return pl.pallas_call(
        _combine_kernel,
        grid_spec=grid_spec,
        out_shape=jax.ShapeDtypeStruct((n, D_MODEL), F32),
        compiler_params=_cparams(("arbitrary",)),
        name="combine",
    )(pos, ys, h1, rw, g, b)


def _rope_tables(pos):
    inv = ROPE_THETA ** (-jnp.arange(0, ROPE_DIM, 2, dtype=F32) / ROPE_DIM)
    ang = pos.astype(F32)[..., None] * inv
    cos, sin = jnp.cos(ang), jnp.sin(ang)
    pad = jnp.zeros(ang.shape[:-1] + (ATT_HEAD_DIM - ROPE_DIM,), F32)
    c64 = jnp.concatenate([cos, cos, pad + 1.0], axis=-1)
    s1_64 = jnp.concatenate([-sin, jnp.zeros_like(sin), pad], axis=-1)
    s2_64 = jnp.concatenate([jnp.zeros_like(sin), sin, pad], axis=-1)
    dup = lambda a: jnp.concatenate([a, a], axis=-1)
    return dup(c64), dup(s1_64), dup(s2_64)


def _expand_cmp_weights(w1, w2, pe):
    g = ATT_KV_GROUPS
    eye = jnp.eye(g, dtype=F32)
    w1r = w1.reshape(L_CMP, ATT_HEAD_DIM, CMP_HIDDEN)
    w1x = jnp.einsum('ldh,ab->ladbh', w1r, eye).reshape(L_CMP, g * ATT_HEAD_DIM, g * CMP_HIDDEN)
    half = STRIDE_CMP
    top = w1x[:half].reshape(half * g * ATT_HEAD_DIM, g * CMP_HIDDEN).astype(BF16)
    bot = w1x[half:].reshape(half * g * ATT_HEAD_DIM, g * CMP_HIDDEN).astype(BF16)
    w2x = jnp.einsum('hd,ab->ahbd', w2, eye).reshape(g * CMP_HIDDEN, g * ATT_HEAD_DIM).astype(BF16)
    pex = jnp.broadcast_to(pe[:, None, :], (L_CMP, g, ATT_HEAD_DIM))
    pet = pex[:half].reshape(1, half * g * ATT_HEAD_DIM)
    peb = pex[half:].reshape(1, half * g * ATT_HEAD_DIM)
    return top, bot, w2x, pet, peb


def _overlap_matrix(nc_rows, seq):
    n_cmp = (seq - L_CMP) // STRIDE_CMP + 1
    cs = np.arange(nc_rows) * STRIDE_CMP
    ss = np.arange(LANES) * L_SEL
    ov = ((cs[:, None] < ss[None, :] + L_SEL) & (ss[None, :] < cs[:, None] + L_CMP)
          & (np.arange(nc_rows)[:, None] < n_cmp) & (ss[None, :] < seq))
    return jnp.asarray(ov.astype(np.float32), dtype=BF16)


def _route_tables(top_i, n_rows_pad, bm):
    a = top_i.shape[0] * TOP_K
    e_flat = top_i.reshape(a)
    onehot = (e_flat[:, None] == jnp.arange(N_EXPERTS, dtype=jnp.int32)[None, :]).astype(jnp.int32)
    csum = jnp.cumsum(onehot, axis=0)
    counts = csum[-1]
    padded = (counts + bm - 1) // bm * bm
    pends = jnp.cumsum(padded)
    pstarts = pends - padded
    pos = jnp.sum(onehot * (csum - 1 + pstarts[None, :]), axis=1).astype(jnp.int32)
    nblk = n_rows_pad // bm
    blk_expert = jnp.clip(jnp.searchsorted(pends, jnp.arange(nblk, dtype=jnp.int32) * bm, side='right'),
                          0, N_EXPERTS - 1).astype(jnp.int32)
    tok = jnp.arange(a, dtype=jnp.int32) // TOP_K
    row_tok = jnp.zeros((n_rows_pad,), jnp.int32).at[pos].set(tok, unique_indices=True)
    return pos, row_tok, blk_expert


def kernel(x, positions, w_in, pe_cmp, w_ck1, w_ck2, w_cv1, w_cv2, hg_lb, hg_norm_g, w_o, ln1_g,
           ln1_b, w_router, b_router, w_gate_up, b_gate_up, w_down, b_down, ln2_g, ln2_b):
    batch, seq, d = x.shape
    n = batch * seq
    assert d == D_MODEL and seq % SEL_TK == 0 and seq % HG_TC == 0 and n % IN_TM == 0
    assert seq // L_SEL <= LANES and seq >= WINDOW + Q_BLOCK
    l = 0
    x2 = x.reshape(n, d)

    n_att = ATT_WIDTH + 6 * KV_WIDTH
    w_pad = jnp.concatenate(
        [w_in[l][:, :n_att + N_GATE], jnp.zeros((d, LANES - N_GATE), F32), w_in[l][:, n_att + N_GATE:]],
        axis=1).astype(BF16)
    wkt, wkb, wk2, pet, peb = _expand_cmp_weights(w_ck1[l], w_ck2[l], pe_cmp[l])
    wvt, wvb, wv2, _, _ = _expand_cmp_weights(w_cv1[l], w_cv2[l], pe_cmp[l])
    lbs = jnp.cumsum(jax.nn.softmax(hg_lb.astype(F32), axis=0), axis=0)[l].reshape(1, HG_HEADS * HG_DK)
    wr_pad = jnp.concatenate([w_router[l], jnp.zeros((d, LANES - N_EXPERTS), F32)], axis=1)
    wr3 = jnp.stack(_split3(wr_pad))
    br_pad = jnp.concatenate([b_router[l], jnp.full((LANES - N_EXPERTS,), NEG, F32)]).reshape(1, LANES)

    c_tab, s1_tab, s2_tab = _rope_tables(positions.reshape(n))
    nc_rows = seq // STRIDE_CMP
    cmp_end = jnp.minimum(jnp.arange(nc_rows) * STRIDE_CMP + L_CMP - 1, seq - 1)
    cc, s1c, s2c = _rope_tables(positions[:, cmp_end])

    qx, kcmp, vcmp, kva, gate, hg = _in_proj(x2, w_pad, c_tab, s1_tab, s2_tab)
    kc, vc = _compress(kcmp.reshape(batch, nc_rows, STRIDE_CMP * LANES),
                       vcmp.reshape(batch, nc_rows, STRIDE_CMP * LANES),
                       pet, peb, wkt, wkb, wvt, wvb, wk2, wv2, cc, s1c, s2c)
    att = _nsa(qx, gate, kc, vc, kva, _overlap_matrix(nc_rows, seq), batch, seq)
    rec = _hgrn2(hg, lbs, hg_norm_g[l].reshape(1, HG_DV), batch, seq)
    h1, ri, rw = _out_proj(att, rec, x2, w_o[l].astype(BF16), ln1_g[l].reshape(1, d),
                           ln1_b[l].reshape(1, d), wr3, br_pad)

    bm = MOE_BM
    a = n * TOP_K
    nblk = -(-(a + N_EXPERTS * (bm - 1)) // bm)
    pos, row_tok, blk_expert = _route_tables(ri[:, :TOP_K], nblk * bm, bm)
    ys = _experts(blk_expert, row_tok, h1, w_gate_up[l].astype(BF16),
                  b_gate_up[l].reshape(N_EXPERTS, 1, 2 * D_FF), w_down[l].astype(BF16),
                  b_down[l].reshape(N_EXPERTS, 1, d))
    out = _combine(pos, ys, h1, rw, ln2_g[l].reshape(1, d), ln2_b[l].reshape(1, d))
    return out.reshape(batch, seq, d)
```

```python
import functools

import numpy as np
import jax
import jax.numpy as jnp
from jax import lax
from jax.experimental import pallas as pl
from jax.experimental.pallas import tpu as pltpu

F32 = jnp.float32
BF16 = jnp.bfloat16

D_MODEL = 1024
ATT_HEADS = 8
ATT_HEAD_DIM = 64
ATT_KV_GROUPS = 2
ATT_HPG = ATT_HEADS // ATT_KV_GROUPS
ATT_WIDTH = ATT_HEADS * ATT_HEAD_DIM
KV_WIDTH = ATT_KV_GROUPS * ATT_HEAD_DIM
N_BRANCH = 3
L_CMP = 32
STRIDE_CMP = 16
CMP_HIDDEN = 256
L_SEL = 64
N_SELECT = 16
N_FORCED_LOCAL = 2
FORCE_BONUS = 1000.0
WINDOW = 512
Q_BLOCK = 128
ROPE_THETA = 500000.0
ROPE_DIM = ATT_HEAD_DIM // 4
ROPE_HALF = ROPE_DIM // 2
HG_HEADS = 4
HG_DK = 128
HG_DV = 128
HG_WIDTH = HG_HEADS * HG_DV
HG_CHUNK = 64
HG_SUB = 16
N_EXPERTS = 32
TOP_K = 4
D_FF = 1024
SWIGLU_LIMIT = 7.0
SWIGLU_ALPHA = 1.702
DEPTH = 1
DN_ALPHA = (2 * DEPTH) ** 0.25
LN_EPS = 1e-5
RMS_EPS = 1e-6
NEG = -1e30
LOG2E = 1.4426950408889634
SEL_SHIFT = L_SEL.bit_length() - 1
SUB_SHIFT = HG_SUB.bit_length() - 1

LANES = 128
VMEM_LIMIT = 56 * 1024 * 1024
IN_TM = 512
SEL_TK = 512
HG_TC = 512
MOE_BM = 256
CMB_TM = 256
GATHER_UNROLL = 8

C_Q = 0
C_KCMP = 512
C_VCMP = 640
C_KVA = 768
C_GATE = 1280
C_HG = 1408
IN_COLS = C_HG + 4 * HG_WIDTH
N_GATE = ATT_HEADS * N_BRANCH


def _nt(a, b):
    return lax.dot_general(a, b, (((1,), (1,)), ((), ())), preferred_element_type=F32)


def _tn(a, b):
    return lax.dot_general(a, b, (((0,), (0,)), ((), ())), preferred_element_type=F32)


def _mm(a, b):
    return jnp.dot(a, b, preferred_element_type=F32)


def _split3(x):
    hi = x.astype(BF16)
    r1 = x - hi.astype(F32)
    mid = r1.astype(BF16)
    lo = (r1 - mid.astype(F32)).astype(BF16)
    return hi, mid, lo


def _rope(v, c, s1, s2):
    return v * c + pltpu.roll(v, LANES - ROPE_HALF, 1) * s1 + pltpu.roll(v, ROPE_HALF, 1) * s2


def _cparams(sem):
    return pltpu.CompilerParams(dimension_semantics=sem, vmem_limit_bytes=VMEM_LIMIT)


def _in_proj_kernel(x_ref, w_ref, c_ref, s1_ref, s2_ref,
                    qx_ref, kcmp_ref, vcmp_ref, kva_ref, gate_ref, hg_ref):
    xb = x_ref[...].astype(BF16)
    c, s1, s2 = c_ref[...], s1_ref[...], s2_ref[...]
    lane = lax.broadcasted_iota(jnp.int32, (1, LANES), 1)
    low = lane < ATT_HEAD_DIM

    q = _mm(xb, w_ref[:, C_Q:C_Q + ATT_WIDTH])
    scale = ATT_HEAD_DIM ** -0.5 * LOG2E
    for j in range(ATT_WIDTH // LANES):
        v = _rope(q[:, j * LANES:(j + 1) * LANES], c, s1, s2) * scale
        vr = pltpu.roll(v, ATT_HEAD_DIM, 1)
        g = (2 * j) // ATT_HPG
        keep = low if g == 0 else jnp.logical_not(low)
        h0 = jnp.where(keep, v if g == 0 else vr, 0.0)
        h1 = jnp.where(keep, vr if g == 0 else v, 0.0)
        qx_ref[:, (2 * j) * LANES:(2 * j + 1) * LANES] = h0.astype(BF16)
        qx_ref[:, (2 * j + 1) * LANES:(2 * j + 2) * LANES] = h1.astype(BF16)

    kv = _mm(xb, w_ref[:, C_KCMP:C_GATE])
    kcmp_ref[...] = kv[:, 0:128]
    vcmp_ref[...] = kv[:, 128:256]
    kva_ref[:, 0:128] = _rope(kv[:, 256:384], c, s1, s2).astype(BF16)
    kva_ref[:, 128:256] = kv[:, 384:512].astype(BF16)
    kva_ref[:, 256:384] = _rope(kv[:, 512:640], c, s1, s2).astype(BF16)
    kva_ref[:, 384:512] = kv[:, 640:768].astype(BF16)

    gate_ref[...] = jax.nn.sigmoid(_mm(xb, w_ref[:, C_GATE:C_HG]))
    for j in range(4):
        hg_ref[:, j * HG_WIDTH:(j + 1) * HG_WIDTH] = _mm(
            xb, w_ref[:, C_HG + j * HG_WIDTH:C_HG + (j + 1) * HG_WIDTH])


def _in_proj(x2, w_pad, c_tab, s1_tab, s2_tab):
    n = x2.shape[0]
    tm = IN_TM
    row = lambda w: pl.BlockSpec((tm, w), lambda i: (i, 0))
    return pl.pallas_call(
        _in_proj_kernel,
        grid=(n // tm,),
        in_specs=[row(D_MODEL), pl.BlockSpec((D_MODEL, IN_COLS), lambda i: (0, 0)),
                  row(LANES), row(LANES), row(LANES)],
        out_specs=[row(ATT_HEADS * LANES), row(LANES), row(LANES), row(4 * LANES), row(LANES),
                   row(4 * HG_WIDTH)],
        out_shape=[jax.ShapeDtypeStruct((n, ATT_HEADS * LANES), BF16),
                   jax.ShapeDtypeStruct((n, LANES), F32),
                   jax.ShapeDtypeStruct((n, LANES), F32),
                   jax.ShapeDtypeStruct((n, 4 * LANES), BF16),
                   jax.ShapeDtypeStruct((n, LANES), F32),
                   jax.ShapeDtypeStruct((n, 4 * HG_WIDTH), F32)],
        compiler_params=_cparams(("arbitrary",)),
        name="in_proj",
    )(x2, w_pad, c_tab, s1_tab, s2_tab)


def _compress_kernel(kf_ref, vf_ref, pet_ref, peb_ref, wkt_ref, wkb_ref, wvt_ref, wvb_ref,
                     wk2_ref, wv2_ref, c_ref, s1_ref, s2_ref, kc_ref, vc_ref):
    nc = kf_ref.shape[0]

    def mlp(x, wt, wb, w2):
        top = (x + pet_ref[...]).astype(BF16)
        bot = (x + peb_ref[...]).astype(BF16)
        u = _mm(top, wt[...])
        v = _mm(bot, wb[...])
        h = u + pltpu.roll(v, nc - 1, 0)
        return _mm(jax.nn.gelu(h).astype(BF16), w2[...])

    kc = mlp(kf_ref[...], wkt_ref, wkb_ref, wk2_ref)
    kc_ref[...] = _rope(kc, c_ref[...], s1_ref[...], s2_ref[...]).astype(BF16)
    vc_ref[...] = mlp(vf_ref[...], wvt_ref, wvb_ref, wv2_ref).astype(BF16)


def _compress(kf, vf, pet, peb, wkt, wkb, wvt, wvb, wk2, wv2, cc, s1c, s2c):
    b, nc, w = kf.shape
    per_b = lambda width: pl.BlockSpec((None, nc, width), lambda i: (i, 0, 0))
    full = lambda a: pl.BlockSpec(a.shape, lambda i: (0,) * a.ndim)
    return pl.pallas_call(
        _compress_kernel,
        grid=(b,),
        in_specs=[per_b(w), per_b(w), full(pet), full(peb), full(wkt), full(wkb), full(wvt),
                  full(wvb), full(wk2), full(wv2), per_b(LANES), per_b(LANES), per_b(LANES)],
        out_specs=[per_b(LANES), per_b(LANES)],
        out_shape=[jax.ShapeDtypeStruct((b, nc, LANES), BF16)] * 2,
        compiler_params=_cparams(("arbitrary",)),
        name="compress",
    )(kf, vf, pet, peb, wkt, wkb, wvt, wvb, wk2, wv2, cc, s1c, s2c)


def _nsa_kernel(qx_ref, gate_ref, kc_ref, vc_ref, kva_ref, ovl_ref, ene_ref, out_ref,
                m_ref, acc_ref, *, seq):
    t = Q_BLOCK
    rows = ATT_HPG * t
    qb = pl.program_id(1)
    s0 = qb * t
    t_pos = s0 + lax.broadcasted_iota(jnp.int32, (t, 1), 0)
    nc = kc_ref.shape[0]
    gates = gate_ref[...]
    lane = lax.broadcasted_iota(jnp.int32, (1, LANES), 1)

    def tile4(a):
        return jnp.concatenate([a] * ATT_HPG, axis=0)

    def per_head(fn, s):
        return jnp.concatenate([fn(s[n * t:(n + 1) * t]) for n in range(ATT_HPG)], axis=0)

    def softmax_pv(s, vx, den_lane):
        m = jnp.max(s, axis=-1, keepdims=True)
        p = jnp.exp2(s - m).astype(BF16)
        o = _mm(p, vx)
        inv = 1.0 / jnp.maximum(o[:, den_lane:den_lane + 1], 1e-20)
        return p, o, inv

    cmp_end = lax.broadcasted_iota(jnp.int32, (1, nc), 1) * STRIDE_CMP + (L_CMP - 1)
    bias_c = jnp.where((cmp_end <= t_pos) & (cmp_end < seq), 0.0, NEG)
    has_cmp = tile4(t_pos >= L_CMP - 1)
    wlen = WINDOW + t
    ws = pl.multiple_of(jnp.maximum(s0 - WINDOW, 0), t)
    kp = ws + lax.broadcasted_iota(jnp.int32, (1, wlen), 1)
    bias_w = jnp.where((kp <= t_pos) & (kp > t_pos - WINDOW), 0.0, NEG)
    dist = jnp.right_shift(t_pos, SEL_SHIFT) - lane
    causal_blk = dist >= 0
    bonus = jnp.where((lane == 0) | (causal_blk & (dist < N_FORCED_LOCAL)), FORCE_BONUS, 0.0)
    brow = lax.broadcasted_iota(jnp.int32, (LANES, t), 0)

    def with_ones(v, g):
        in_g = (lane >= g * ATT_HEAD_DIM) & (lane < (g + 1) * ATT_HEAD_DIM)
        return v * jnp.where(in_g, 1.0, 0.0).astype(BF16) + jnp.where(in_g, 0.0, 1.0).astype(BF16)

    groups = range(ATT_KV_GROUPS)
    den = [(1 - g) * ATT_HEAD_DIM for g in groups]
    qgs, o_cs, lhss = [], [], []
    for g in groups:
        qg = jnp.concatenate(
            [qx_ref[:, (g * ATT_HPG + n) * LANES:(g * ATT_HPG + n + 1) * LANES]
             for n in range(ATT_HPG)], axis=0)
        qgs.append(qg)

        s_c = per_head(lambda a: a + bias_c, _nt(qg, kc_ref[...]))
        p_c, o_c, inv_c = softmax_pv(s_c, with_ones(vc_ref[...], g), den[g])
        inv_c = jnp.where(has_cmp, inv_c, 0.0)
        o_cs.append(o_c * inv_c)
        imp4 = _mm(p_c, ovl_ref[...]) * inv_c
        imp = imp4[0:t]
        for n in range(1, ATT_HPG):
            imp = imp + imp4[n * t:(n + 1) * t]

        imp_t = jnp.where(causal_blk, imp + bonus, -1.0).T
        sel_t = jnp.zeros((LANES, t), F32)
        for _ in range(N_SELECT):
            mx = jnp.max(imp_t, axis=0, keepdims=True)
            idx = jnp.min(jnp.where(imp_t == mx, brow, LANES), axis=0, keepdims=True)
            pick = brow == idx
            sel_t = jnp.where(pick, 1.0, sel_t)
            imp_t = jnp.where(pick, -3e38, imp_t)
        unsel = jnp.where(causal_blk, 1.0 - sel_t.T, 1.0).astype(BF16)

        lhss.append(jnp.concatenate([qg, tile4(unsel)], axis=1))

    lhs = jnp.concatenate(lhss, axis=0)
    m_ref[...] = jnp.full(m_ref.shape, NEG, F32)
    acc_ref[...] = jnp.zeros(acc_ref.shape, F32)

    def sel_tile(kt, diagonal):
        k0 = pl.multiple_of(kt * SEL_TK, SEL_TK)
        rhs = jnp.concatenate([kva_ref[pl.ds(k0, SEL_TK), 0:128],
                               ene_ref[pl.ds(k0, SEL_TK), :]], axis=1)
        v_t = kva_ref[pl.ds(k0, SEL_TK), 128:256]
        s = _nt(lhs, rhs)
        if diagonal:
            kpos = k0 + lax.broadcasted_iota(jnp.int32, (1, SEL_TK), 1)
            future = kpos > t_pos
            s = jnp.concatenate([jnp.where(future, NEG, s[n * t:(n + 1) * t])
                                 for n in range(ATT_HEADS)], axis=0)
        m_old = m_ref[...]
        m_new = jnp.maximum(m_old, jnp.max(s, axis=-1, keepdims=True))
        m_ref[...] = m_new
        p = jnp.concatenate([s[:, j * LANES:(j + 1) * LANES] - m_new
                             for j in range(SEL_TK // LANES)], axis=1)
        p = jnp.exp2(p).astype(BF16)
        pv = jnp.concatenate([_mm(p[g * rows:(g + 1) * rows], with_ones(v_t, g)) for g in groups],
                             axis=0)
        acc_ref[...] = jnp.exp2(m_old - m_new) * acc_ref[...] + pv

    n_tiles = (s0 + t + SEL_TK - 1) // SEL_TK

    def full_tile(kt, carry):
        sel_tile(kt, False)
        return carry

    lax.fori_loop(0, n_tiles - 1, full_tile, 0)
    sel_tile(n_tiles - 1, True)

    pieces = []
    for g in groups:
        acc = acc_ref[g * rows:(g + 1) * rows, :]
        o_s = acc * (1.0 / jnp.maximum(acc[:, den[g]:den[g] + 1], 1e-20))

        s_w = per_head(lambda a: a + bias_w, _nt(qgs[g], kva_ref[pl.ds(ws, wlen), 256:384]))
        _, o_w, inv_w = softmax_pv(s_w, with_ones(kva_ref[pl.ds(ws, wlen), 384:512], g), den[g])
        o_w = o_w * inv_w

        o_c = o_cs[g]
        for n in range(ATT_HPG):
            c0 = (g * ATT_HPG + n) * N_BRANCH
            r = slice(n * t, (n + 1) * t)
            ln = slice(g * ATT_HEAD_DIM, (g + 1) * ATT_HEAD_DIM)
            pieces.append(gates[:, c0:c0 + 1] * o_c[r, ln]
                          + gates[:, c0 + 1:c0 + 2] * o_s[r, ln]
                          + gates[:, c0 + 2:c0 + 3] * o_w[r, ln])
    out_ref[...] = jnp.concatenate(pieces, axis=1).astype(BF16)


def _nsa(qx, gate, kc, vc, kva, ovl, ene, batch, seq):
    n = qx.shape[0]
    nqb = seq // Q_BLOCK
    nc = kc.shape[1]
    row = lambda w: pl.BlockSpec((Q_BLOCK, w), lambda b, i: (b * nqb + i, 0))
    return pl.pallas_call(
        functools.partial(_nsa_kernel, seq=seq),
        grid=(batch, nqb),
        in_specs=[row(ATT_HEADS * LANES), row(LANES),
                  pl.BlockSpec((None, nc, LANES), lambda b, i: (b, 0, 0)),
                  pl.BlockSpec((None, nc, LANES), lambda b, i: (b, 0, 0)),
                  pl.BlockSpec((None, seq, 4 * LANES), lambda b, i: (b, 0, 0)),
                  pl.BlockSpec(ovl.shape, lambda b, i: (0, 0)),
                  pl.BlockSpec(ene.shape, lambda b, i: (0, 0))],
        out_specs=row(ATT_WIDTH),
        out_shape=jax.ShapeDtypeStruct((n, ATT_WIDTH), BF16),
        scratch_shapes=[pltpu.VMEM((ATT_HEADS * Q_BLOCK, LANES), F32),
                        pltpu.VMEM((ATT_HEADS * Q_BLOCK, LANES), F32)],
        compiler_params=_cparams(("arbitrary", "arbitrary")),
        name="nsa",
    )(qx, gate, kc, vc, kva.reshape(batch, seq, 4 * LANES), ovl, ene)


def _hgrn2_kernel(q_ref, f_ref, i_ref, g_ref, lb_ref, ng_ref, out_ref, st_ref, logf_ref, kk_ref):
    c = HG_CHUNK
    sub = HG_SUB
    nsub = c // sub

    @pl.when(pl.program_id(2) == 0)
    def _():
        st_ref[...] = jnp.zeros_like(st_ref)

    lb = lb_ref[...]
    f_hat = jax.nn.sigmoid(f_ref[...])
    logf_ref[...] = jnp.log(lb + (1.0 - lb) * f_hat)
    kk_ref[...] = (1.0 - lb) * (1.0 - f_hat)

    ri = lax.broadcasted_iota(jnp.int32, (c, c), 0)
    ci = lax.broadcasted_iota(jnp.int32, (c, c), 1)
    tril = jnp.where(ci <= ri, 1.0, 0.0).astype(BF16)
    row = lax.broadcasted_iota(jnp.int32, (c, 1), 0)
    row_sub = jnp.right_shift(row, SUB_SHIFT)
    row_loc = row - row_sub * sub
    ng = ng_ref[...]

    def chunk(ic, carry):
        r0 = pl.multiple_of(ic * c, c)
        q = q_ref[pl.ds(r0, c), :]
        k = kk_ref[pl.ds(r0, c), :]
        v = i_ref[pl.ds(r0, c), :].astype(BF16)
        lf_hi, lf_mid, lf_lo = _split3(logf_ref[pl.ds(r0, c), :])
        b = _mm(tril, lf_hi) + _mm(tril, lf_mid) + _mm(tril, lf_lo)
        st = st_ref[...]
        o = _nt((q * jnp.exp(b)).astype(BF16), st.astype(BF16))

        a_rows = [jnp.zeros((sub, c), F32)]
        for i in range(1, nsub):
            ref_b = b[i * sub:i * sub + 1, :]
            qs = q[i * sub:(i + 1) * sub, :] * jnp.exp(b[i * sub:(i + 1) * sub, :] - ref_b)
            ks = jnp.where(row < i * sub, k * jnp.exp(jnp.minimum(ref_b - b, 0.0)), 0.0)
            a_rows.append(_nt(qs.astype(BF16), ks.astype(BF16)))
        a = jnp.concatenate(a_rows, axis=0)

        b3 = b.reshape(nsub, sub, HG_DK)
        q3 = q.reshape(nsub, sub, HG_DK)
        k3 = k.reshape(nsub, sub, HG_DK)
        for s in range(sub):
            dec = jnp.exp(jnp.minimum(b3 - b3[:, s:s + 1, :], 0.0))
            w = (q3 * k3[:, s:s + 1, :] * dec).reshape(c, HG_DK)
            col = jnp.sum(w, axis=-1, keepdims=True)
            hit = (ci == row_sub * sub + s) & (row_loc >= s)
            a = a + jnp.where(hit, col, 0.0)

        o = o + _mm(a.astype(BF16), v)
        b_last = b[c - 1:c, :]
        kd = (k * jnp.exp(b_last - b)).astype(BF16)
        st_ref[...] = st * jnp.exp(b_last) + _tn(v, kd)

        o = o * lax.rsqrt(jnp.mean(o * o, axis=-1, keepdims=True) + RMS_EPS) * ng
        out_ref[pl.ds(r0, c), :] = (o * jax.nn.silu(g_ref[pl.ds(r0, c), :])).astype(BF16)
        return carry

    lax.fori_loop(0, q_ref.shape[0] // c, chunk, 0)


def _hgrn2(hg, lb, ng, batch, seq):
    n = hg.shape[0]
    tc = HG_TC
    nt = seq // tc
    col = lambda j: pl.BlockSpec((tc, HG_DK), lambda b, h, i, j=j: (b * nt + i, j * HG_HEADS + h))
    return pl.pallas_call(
        _hgrn2_kernel,
        grid=(batch, HG_HEADS, nt),
        in_specs=[col(0), col(1), col(2), col(3),
                  pl.BlockSpec((1, HG_DK), lambda b, h, i: (0, h)),
                  pl.BlockSpec((1, HG_DV), lambda b, h, i: (0, 0))],
        out_specs=pl.BlockSpec((tc, HG_DV), lambda b, h, i: (b * nt + i, h)),
        out_shape=jax.ShapeDtypeStruct((n, HG_WIDTH), BF16),
        scratch_shapes=[pltpu.VMEM((HG_DV, HG_DK), F32),
                        pltpu.VMEM((tc, HG_DK), F32),
                        pltpu.VMEM((tc, HG_DK), F32)],
        compiler_params=_cparams(("arbitrary", "arbitrary", "arbitrary")),
        name="hgrn2",
    )(hg, hg, hg, hg, lb, ng)


def _layer_norm(y, g, b):
    mu = jnp.mean(y, axis=-1, keepdims=True)
    d = y - mu
    var = jnp.mean(d * d, axis=-1, keepdims=True)
    return d * lax.rsqrt(var + LN_EPS) * g + b


def _out_proj_kernel(att_ref, rec_ref, x_ref, wo_ref, g_ref, b_ref, wr_ref, br_ref,
                     h_ref, ri_ref, rw_ref):
    mix = _mm(att_ref[...], wo_ref[0:ATT_WIDTH, :]) + _mm(rec_ref[...], wo_ref[ATT_WIDTH:, :])
    h = _layer_norm(DN_ALPHA * x_ref[...] + mix, g_ref[...], b_ref[...])
    h_ref[...] = h

    hh, hm, hl = _split3(h)
    wh, wm, wl = wr_ref[0], wr_ref[1], wr_ref[2]
    lg = (_mm(hh, wh) + _mm(hh, wm) + _mm(hm, wh) + _mm(hh, wl) + _mm(hl, wh) + _mm(hm, wm)
          + br_ref[...])
    lane = lax.broadcasted_iota(jnp.int32, lg.shape, 1)
    ri = jnp.zeros(lg.shape, jnp.int32)
    rw = jnp.zeros(lg.shape, F32)
    v0 = None
    for r in range(TOP_K):
        mx = jnp.max(lg, axis=-1, keepdims=True)
        idx = jnp.min(jnp.where(lg == mx, lane, LANES), axis=-1, keepdims=True)
        if r == 0:
            v0 = mx
        ri = jnp.where(lane == r, idx, ri)
        rw = jnp.where(lane == r, jnp.exp(mx - v0), rw)
        lg = jnp.where(lane == idx, -3e38, lg)
    ri_ref[...] = ri
    rw_ref[...] = rw / jnp.sum(rw, axis=-1, keepdims=True)


def _out_proj(att, rec, x2, wo, g, b, wr3, br):
    n = x2.shape[0]
    tm = IN_TM
    row = lambda w: pl.BlockSpec((tm, w), lambda i: (i, 0))
    full = lambda a: pl.BlockSpec(a.shape, lambda i: (0,) * a.ndim)
    return pl.pallas_call(
        _out_proj_kernel,
        grid=(n // tm,),
        in_specs=[row(ATT_WIDTH), row(HG_WIDTH), row(D_MODEL), full(wo), full(g), full(b),
                  full(wr3), full(br)],
        out_specs=[row(D_MODEL), row(LANES), row(LANES)],
        out_shape=[jax.ShapeDtypeStruct((n, D_MODEL), F32),
                   jax.ShapeDtypeStruct((n, LANES), jnp.int32),
                   jax.ShapeDtypeStruct((n, LANES), F32)],
        compiler_params=_cparams(("arbitrary",)),
        name="out_proj",
    )(att, rec, x2, wo, g, b, wr3, br)


def _experts_kernel(be_ref, tok_ref, h_hbm, wgu_ref, bgu_ref, wdn_ref, bdn_ref, y_ref,
                    xbuf, sem):
    i = pl.program_id(0)
    nblk = pl.num_programs(0)
    bm = xbuf.shape[1]

    def gather(blk, slot):
        def issue(r, carry):
            tok = tok_ref[blk * bm + r]
            pltpu.make_async_copy(h_hbm.at[pl.ds(tok, 1), :], xbuf.at[slot, pl.ds(r, 1), :],
                                  sem.at[slot]).start()
            return carry
        lax.fori_loop(0, bm, issue, 0, unroll=GATHER_UNROLL)

    @pl.when(i == 0)
    def _():
        gather(0, 0)

    @pl.when(i + 1 < nblk)
    def _():
        gather(i + 1, (i + 1) % 2)

    slot = i % 2
    pltpu.make_async_copy(h_hbm.at[pl.ds(0, bm), :], xbuf.at[slot], sem.at[slot]).wait()

    xb = xbuf[slot].astype(BF16)
    h = _mm(xb, wgu_ref[...]) + bgu_ref[...]
    gate = jnp.minimum(h[:, :D_FF], SWIGLU_LIMIT)
    up = jnp.clip(h[:, D_FF:], -SWIGLU_LIMIT, SWIGLU_LIMIT)
    act = gate * jax.nn.sigmoid(SWIGLU_ALPHA * gate) * (up + 1.0)
    y_ref[...] = _mm(act.astype(BF16), wdn_ref[...]) + bdn_ref[...]


def _experts(blk_expert, row_tok, h1, wgu, bgu, wdn, bdn):
    nblk = blk_expert.shape[0]
    bm = MOE_BM
    grid_spec = pltpu.PrefetchScalarGridSpec(
        num_scalar_prefetch=2,
        grid=(nblk,),
        in_specs=[pl.BlockSpec(memory_space=pl.ANY),
                  pl.BlockSpec((None, D_MODEL, 2 * D_FF), lambda i, be, tok: (be[i], 0, 0)),
                  pl.BlockSpec((None, 1, 2 * D_FF), lambda i, be, tok: (be[i], 0, 0)),
                  pl.BlockSpec((None, D_FF, D_MODEL), lambda i, be, tok: (be[i], 0, 0)),
                  pl.BlockSpec((None, 1, D_MODEL), lambda i, be, tok: (be[i], 0, 0))],
        out_specs=pl.BlockSpec((bm, D_MODEL), lambda i, be, tok: (i, 0)),
        scratch_shapes=[pltpu.VMEM((2, bm, D_MODEL), F32), pltpu.SemaphoreType.DMA((2,))],
    )
    return pl.pallas_call(
        _experts_kernel,
        grid_spec=grid_spec,
        out_shape=jax.ShapeDtypeStruct((nblk * bm, D_MODEL), F32),
        compiler_params=_cparams(("arbitrary",)),
        name="experts",
    )(blk_expert, row_tok, h1, wgu, bgu, wdn, bdn)


def _combine_kernel(pos_ref, y_hbm, h_ref, rw_ref, g_ref, b_ref, out_ref, ybuf, sem):
    i = pl.program_id(0)
    nstep = pl.num_programs(0)
    tm = ybuf.shape[2]

    def gather(step, slot):
        def issue(r, carry):
            for k in range(TOP_K):
                p = pos_ref[(step * tm + r) * TOP_K + k]
                pltpu.make_async_copy(y_hbm.at[pl.ds(p, 1), :], ybuf.at[slot, k, pl.ds(r, 1), :],
                                      sem.at[slot]).start()
            return carry
        lax.fori_loop(0, tm, issue, 0, unroll=GATHER_UNROLL // TOP_K)

    @pl.when(i == 0)
    def _():
        gather(0, 0)

    @pl.when(i + 1 < nstep)
    def _():
        gather(i + 1, (i + 1) % 2)

    slot = i % 2
    for k in range(TOP_K):
        pltpu.make_async_copy(y_hbm.at[pl.ds(0, tm), :], ybuf.at[slot, k], sem.at[slot]).wait()

    rw = rw_ref[...]
    ffn = rw[:, 0:1] * ybuf[slot, 0]
    for k in range(1, TOP_K):
        ffn = ffn + rw[:, k:k + 1] * ybuf[slot, k]
    out_ref[...] = _layer_norm(DN_ALPHA * h_ref[...] + ffn, g_ref[...], b_ref[...])


def _combine(pos, ys, h1, rw, g, b):
    n = h1.shape[0]
    tm = CMB_TM
    grid_spec = pltpu.PrefetchScalarGridSpec(
        num_scalar_prefetch=1,
        grid=(n // tm,),
        in_specs=[pl.BlockSpec(memory_space=pl.ANY),
                  pl.BlockSpec((tm, D_MODEL), lambda i, pos: (i, 0)),
                  pl.BlockSpec((tm, LANES), lambda i, pos: (i, 0)),
                  pl.BlockSpec((1, D_MODEL), lambda i, pos: (0, 0)),
                  pl.BlockSpec((1, D_MODEL), lambda i, pos: (0, 0))],
        out_specs=pl.BlockSpec((tm, D_MODEL), lambda i, pos: (i, 0)),
        scratch_shapes=[pltpu.VMEM((2, TOP_K, tm, D_MODEL), F32), pltpu.SemaphoreType.DMA((2,))],
    )
    return pl.pallas_call(
        _combine_kernel,
        grid_spec=grid_spec,
        out_shape=jax.ShapeDtypeStruct((n, D_MODEL), F32),
        compiler_params=_cparams(("arbitrary",)),
        name="combine",
    )(pos, ys, h1, rw, g, b)


def _rope_tables(pos):
    inv = ROPE_THETA ** (-jnp.arange(0, ROPE_DIM, 2, dtype=F32) / ROPE_DIM)
    ang = pos.astype(F32)[..., None] * inv
    cos, sin = jnp.cos(ang), jnp.sin(ang)
    pad = jnp.zeros(ang.shape[:-1] + (ATT_HEAD_DIM - ROPE_DIM,), F32)
    c64 = jnp.concatenate([cos, cos, pad + 1.0], axis=-1)
    s1_64 = jnp.concatenate([-sin, jnp.zeros_like(sin), pad], axis=-1)
    s2_64 = jnp.concatenate([jnp.zeros_like(sin), sin, pad], axis=-1)
    dup = lambda a: jnp.concatenate([a, a], axis=-1)
    return dup(c64), dup(s1_64), dup(s2_64)


def _expand_cmp_weights(w1, w2, pe):
    g = ATT_KV_GROUPS
    eye = jnp.eye(g, dtype=F32)
    w1r = w1.reshape(L_CMP, ATT_HEAD_DIM, CMP_HIDDEN)
    w1x = jnp.einsum('ldh,ab->ladbh', w1r, eye).reshape(L_CMP, g * ATT_HEAD_DIM, g * CMP_HIDDEN)
    half = STRIDE_CMP
    top = w1x[:half].reshape(half * g * ATT_HEAD_DIM, g * CMP_HIDDEN).astype(BF16)
    bot = w1x[half:].reshape(half * g * ATT_HEAD_DIM, g * CMP_HIDDEN).astype(BF16)
    w2x = jnp.einsum('hd,ab->ahbd', w2, eye).reshape(g * CMP_HIDDEN, g * ATT_HEAD_DIM).astype(BF16)
    pex = jnp.broadcast_to(pe[:, None, :], (L_CMP, g, ATT_HEAD_DIM))
    pet = pex[:half].reshape(1, half * g * ATT_HEAD_DIM)
    peb = pex[half:].reshape(1, half * g * ATT_HEAD_DIM)
    return top, bot, w2x, pet, peb


def _overlap_matrix(nc_rows, seq):
    n_cmp = (seq - L_CMP) // STRIDE_CMP + 1
    cs = np.arange(nc_rows) * STRIDE_CMP
    ss = np.arange(LANES) * L_SEL
    ov = ((cs[:, None] < ss[None, :] + L_SEL) & (ss[None, :] < cs[:, None] + L_CMP)
          & (np.arange(nc_rows)[:, None] < n_cmp) & (ss[None, :] < seq))
    return jnp.asarray(ov.astype(np.float32), dtype=BF16)


def _block_mask_keys(seq):
    hot = (np.arange(seq)[:, None] // L_SEL) == np.arange(LANES)[None, :]
    return jnp.asarray(np.where(hot, NEG, 0.0).astype(np.float32), dtype=BF16)


def _route_tables(top_i, n_rows_pad, bm):
    a = top_i.shape[0] * TOP_K
    e_flat = top_i.reshape(a)
    onehot = (e_flat[:, None] == jnp.arange(N_EXPERTS, dtype=jnp.int32)[None, :]).astype(jnp.int32)
    csum = jnp.cumsum(onehot, axis=0)
    counts = csum[-1]
    padded = (counts + bm - 1) // bm * bm
    pends = jnp.cumsum(padded)
    pstarts = pends - padded
    pos = jnp.sum(onehot * (csum - 1 + pstarts[None, :]), axis=1).astype(jnp.int32)
    nblk = n_rows_pad // bm
    blk_start = jnp.arange(nblk, dtype=jnp.int32) * bm
    blk_expert = jnp.minimum(jnp.sum((pends[None, :] <= blk_start[:, None]).astype(jnp.int32), axis=1),
                             N_EXPERTS - 1)
    tok = jnp.arange(a, dtype=jnp.int32) // TOP_K
    row_tok = jnp.zeros((n_rows_pad,), jnp.int32).at[pos].set(tok, unique_indices=True)
    return pos, row_tok, blk_expert


def kernel(x, positions, w_in, pe_cmp, w_ck1, w_ck2, w_cv1, w_cv2, hg_lb, hg_norm_g, w_o, ln1_g,
           ln1_b, w_router, b_router, w_gate_up, b_gate_up, w_down, b_down, ln2_g, ln2_b):
    batch, seq, d = x.shape
    n = batch * seq
    assert d == D_MODEL and seq % SEL_TK == 0 and seq % HG_TC == 0 and n % IN_TM == 0
    assert seq // L_SEL <= LANES and seq >= WINDOW + Q_BLOCK
    l = 0
    x2 = x.reshape(n, d)

    n_att = ATT_WIDTH + 6 * KV_WIDTH
    w_pad = jnp.concatenate(
        [w_in[l][:, :n_att + N_GATE], jnp.zeros((d, LANES - N_GATE), F32), w_in[l][:, n_att + N_GATE:]],
        axis=1).astype(BF16)
    wkt, wkb, wk2, pet, peb = _expand_cmp_weights(w_ck1[l], w_ck2[l], pe_cmp[l])
    wvt, wvb, wv2, _, _ = _expand_cmp_weights(w_cv1[l], w_cv2[l], pe_cmp[l])
    lbs = jnp.cumsum(jax.nn.softmax(hg_lb.astype(F32), axis=0), axis=0)[l].reshape(1, HG_HEADS * HG_DK)
    wr_pad = jnp.concatenate([w_router[l], jnp.zeros((d, LANES - N_EXPERTS), F32)], axis=1)
    wr3 = jnp.stack(_split3(wr_pad))
    br_pad = jnp.concatenate([b_router[l], jnp.full((LANES - N_EXPERTS,), NEG, F32)]).reshape(1, LANES)

    c_tab, s1_tab, s2_tab = _rope_tables(positions.reshape(n))
    nc_rows = seq // STRIDE_CMP
    cmp_end = jnp.minimum(jnp.arange(nc_rows) * STRIDE_CMP + L_CMP - 1, seq - 1)
    cc, s1c, s2c = _rope_tables(positions[:, cmp_end])

    qx, kcmp, vcmp, kva, gate, hg = _in_proj(x2, w_pad, c_tab, s1_tab, s2_tab)
    kc, vc = _compress(kcmp.reshape(batch, nc_rows, STRIDE_CMP * LANES),
                       vcmp.reshape(batch, nc_rows, STRIDE_CMP * LANES),
                       pet, peb, wkt, wkb, wvt, wvb, wk2, wv2, cc, s1c, s2c)
    att = _nsa(qx, gate, kc, vc, kva, _overlap_matrix(nc_rows, seq), _block_mask_keys(seq),
               batch, seq)
    rec = _hgrn2(hg, lbs, hg_norm_g[l].reshape(1, HG_DV), batch, seq)
    h1, ri, rw = _out_proj(att, rec, x2, w_o[l].astype(BF16), ln1_g[l].reshape(1, d),
                           ln1_b[l].reshape(1, d), wr3, br_pad)

    bm = MOE_BM
    a = n * TOP_K
    nblk = -(-(a + N_EXPERTS * (bm - 1)) // bm)
    pos, row_tok, blk_expert = _route_tables(ri[:, :TOP_K], nblk * bm, bm)
    ys = _experts(blk_expert, row_tok, h1, w_gate_up[l].astype(BF16),
                  b_gate_up[l].reshape(N_EXPERTS, 1, 2 * D_FF), w_down[l].astype(BF16),
                  b_down[l].reshape(N_EXPERTS, 1, d))
    out = _combine(pos, ys, h1, rw, ln2_g[l].reshape(1, d), ln2_b[l].reshape(1, d))
    return out.reshape(batch, seq, d)
```

```python
import functools

import numpy as np
import jax
import jax.numpy as jnp
from jax import lax
from jax.experimental import pallas as pl
from jax.experimental.pallas import tpu as pltpu

F32 = jnp.float32
BF16 = jnp.bfloat16

D_MODEL = 1024
ATT_HEADS = 8
ATT_HEAD_DIM = 64
ATT_KV_GROUPS = 2
ATT_HPG = ATT_HEADS // ATT_KV_GROUPS
ATT_WIDTH = ATT_HEADS * ATT_HEAD_DIM
KV_WIDTH = ATT_KV_GROUPS * ATT_HEAD_DIM
N_BRANCH = 3
L_CMP = 32
STRIDE_CMP = 16
CMP_HIDDEN = 256
L_SEL = 64
N_SELECT = 16
N_FORCED_LOCAL = 2
FORCE_BONUS = 1000.0
WINDOW = 512
Q_BLOCK = 128
ROPE_THETA = 500000.0
ROPE_DIM = ATT_HEAD_DIM // 4
ROPE_HALF = ROPE_DIM // 2
HG_HEADS = 4
HG_DK = 128
HG_DV = 128
HG_WIDTH = HG_HEADS * HG_DV
HG_CHUNK = 64
HG_SUB = 16
N_EXPERTS = 32
TOP_K = 4
D_FF = 1024
SWIGLU_LIMIT = 7.0
SWIGLU_ALPHA = 1.702
DEPTH = 1
DN_ALPHA = (2 * DEPTH) ** 0.25
LN_EPS = 1e-5
RMS_EPS = 1e-6
NEG = -1e30
LOG2E = 1.4426950408889634
SEL_SHIFT = L_SEL.bit_length() - 1
SUB_SHIFT = HG_SUB.bit_length() - 1

LANES = 128
VMEM_LIMIT = 56 * 1024 * 1024
IN_TM = 512
SEL_TK = 512
HG_TC = 512
MOE_BM = 256
CMB_TM = 256
GATHER_UNROLL = 8
ROW_TILE = D_MODEL // LANES
ROUTE_TM = 512

C_Q = 0
C_KCMP = 512
C_VCMP = 640
C_KVA = 768
C_GATE = 1280
C_HG = 1408
IN_COLS = C_HG + 4 * HG_WIDTH
N_GATE = ATT_HEADS * N_BRANCH


def _nt(a, b):
    return lax.dot_general(a, b, (((1,), (1,)), ((), ())), preferred_element_type=F32)


def _tn(a, b):
    return lax.dot_general(a, b, (((0,), (0,)), ((), ())), preferred_element_type=F32)


def _mm(a, b):
    return jnp.dot(a, b, preferred_element_type=F32)


def _split3(x):
    hi = x.astype(BF16)
    r1 = x - hi.astype(F32)
    mid = r1.astype(BF16)
    lo = (r1 - mid.astype(F32)).astype(BF16)
    return hi, mid, lo


def _rope(v, c, s1, s2):
    return v * c + pltpu.roll(v, LANES - ROPE_HALF, 1) * s1 + pltpu.roll(v, ROPE_HALF, 1) * s2


def _cparams(sem):
    return pltpu.CompilerParams(dimension_semantics=sem, vmem_limit_bytes=VMEM_LIMIT)


def _in_proj_kernel(x_ref, w_ref, c_ref, s1_ref, s2_ref,
                    qx_ref, kcmp_ref, vcmp_ref, kva_ref, gate_ref, hg_ref):
    xb = x_ref[...].astype(BF16)
    c, s1, s2 = c_ref[...], s1_ref[...], s2_ref[...]
    lane = lax.broadcasted_iota(jnp.int32, (1, LANES), 1)
    low = lane < ATT_HEAD_DIM

    q = _mm(xb, w_ref[:, C_Q:C_Q + ATT_WIDTH])
    scale = ATT_HEAD_DIM ** -0.5 * LOG2E
    for j in range(ATT_WIDTH // LANES):
        v = _rope(q[:, j * LANES:(j + 1) * LANES], c, s1, s2) * scale
        vr = pltpu.roll(v, ATT_HEAD_DIM, 1)
        g = (2 * j) // ATT_HPG
        keep = low if g == 0 else jnp.logical_not(low)
        h0 = jnp.where(keep, v if g == 0 else vr, 0.0)
        h1 = jnp.where(keep, vr if g == 0 else v, 0.0)
        qx_ref[:, (2 * j) * LANES:(2 * j + 1) * LANES] = h0.astype(BF16)
        qx_ref[:, (2 * j + 1) * LANES:(2 * j + 2) * LANES] = h1.astype(BF16)

    kv = _mm(xb, w_ref[:, C_KCMP:C_GATE])
    kcmp_ref[...] = kv[:, 0:128]
    vcmp_ref[...] = kv[:, 128:256]
    kva_ref[:, 0:128] = _rope(kv[:, 256:384], c, s1, s2).astype(BF16)
    kva_ref[:, 128:256] = kv[:, 384:512].astype(BF16)
    kva_ref[:, 256:384] = _rope(kv[:, 512:640], c, s1, s2).astype(BF16)
    kva_ref[:, 384:512] = kv[:, 640:768].astype(BF16)

    gate_ref[...] = jax.nn.sigmoid(_mm(xb, w_ref[:, C_GATE:C_HG]))
    for j in range(4):
        hg_ref[:, j * HG_WIDTH:(j + 1) * HG_WIDTH] = _mm(
            xb, w_ref[:, C_HG + j * HG_WIDTH:C_HG + (j + 1) * HG_WIDTH])


def _in_proj(x2, w_pad, c_tab, s1_tab, s2_tab):
    n = x2.shape[0]
    tm = IN_TM
    row = lambda w: pl.BlockSpec((tm, w), lambda i: (i, 0))
    return pl.pallas_call(
        _in_proj_kernel,
        grid=(n // tm,),
        in_specs=[row(D_MODEL), pl.BlockSpec((D_MODEL, IN_COLS), lambda i: (0, 0)),
                  row(LANES), row(LANES), row(LANES)],
        out_specs=[row(ATT_HEADS * LANES), row(LANES), row(LANES), row(4 * LANES), row(LANES),
                   row(4 * HG_WIDTH)],
        out_shape=[jax.ShapeDtypeStruct((n, ATT_HEADS * LANES), BF16),
                   jax.ShapeDtypeStruct((n, LANES), F32),
                   jax.ShapeDtypeStruct((n, LANES), F32),
                   jax.ShapeDtypeStruct((n, 4 * LANES), BF16),
                   jax.ShapeDtypeStruct((n, LANES), F32),
                   jax.ShapeDtypeStruct((n, 4 * HG_WIDTH), F32)],
        compiler_params=_cparams(("arbitrary",)),
        name="in_proj",
    )(x2, w_pad, c_tab, s1_tab, s2_tab)


def _compress_kernel(kf_ref, vf_ref, pet_ref, peb_ref, wkt_ref, wkb_ref, wvt_ref, wvb_ref,
                     wk2_ref, wv2_ref, c_ref, s1_ref, s2_ref, kc_ref, vc_ref):
    nc = kf_ref.shape[0]

    def mlp(x, wt, wb, w2):
        top = (x + pet_ref[...]).astype(BF16)
        bot = (x + peb_ref[...]).astype(BF16)
        u = _mm(top, wt[...])
        v = _mm(bot, wb[...])
        h = u + pltpu.roll(v, nc - 1, 0)
        return _mm(jax.nn.gelu(h).astype(BF16), w2[...])

    kc = mlp(kf_ref[...], wkt_ref, wkb_ref, wk2_ref)
    kc_ref[...] = _rope(kc, c_ref[...], s1_ref[...], s2_ref[...]).astype(BF16)
    vc_ref[...] = mlp(vf_ref[...], wvt_ref, wvb_ref, wv2_ref).astype(BF16)


def _compress(kf, vf, pet, peb, wkt, wkb, wvt, wvb, wk2, wv2, cc, s1c, s2c):
    b, nc, w = kf.shape
    per_b = lambda width: pl.BlockSpec((None, nc, width), lambda i: (i, 0, 0))
    full = lambda a: pl.BlockSpec(a.shape, lambda i: (0,) * a.ndim)
    return pl.pallas_call(
        _compress_kernel,
        grid=(b,),
        in_specs=[per_b(w), per_b(w), full(pet), full(peb), full(wkt), full(wkb), full(wvt),
                  full(wvb), full(wk2), full(wv2), per_b(LANES), per_b(LANES), per_b(LANES)],
        out_specs=[per_b(LANES), per_b(LANES)],
        out_shape=[jax.ShapeDtypeStruct((b, nc, LANES), BF16)] * 2,
        compiler_params=_cparams(("arbitrary",)),
        name="compress",
    )(kf, vf, pet, peb, wkt, wkb, wvt, wvb, wk2, wv2, cc, s1c, s2c)


def _nsa_kernel(qx_ref, gate_ref, kc_ref, vc_ref, kva_ref, ovl_ref, ene_ref, out_ref,
                m_ref, acc_ref, *, seq):
    t = Q_BLOCK
    rows = ATT_HPG * t
    qb = pl.program_id(1)
    s0 = qb * t
    t_pos = s0 + lax.broadcasted_iota(jnp.int32, (t, 1), 0)
    nc = kc_ref.shape[0]
    gates = gate_ref[...]
    lane = lax.broadcasted_iota(jnp.int32, (1, LANES), 1)

    def tile4(a):
        return jnp.concatenate([a] * ATT_HPG, axis=0)

    def per_head(fn, s):
        return jnp.concatenate([fn(s[n * t:(n + 1) * t]) for n in range(ATT_HPG)], axis=0)

    def softmax_pv(s, vx, den_lane):
        m = jnp.max(s, axis=-1, keepdims=True)
        p = jnp.exp2(s - m).astype(BF16)
        o = _mm(p, vx)
        inv = 1.0 / jnp.maximum(o[:, den_lane:den_lane + 1], 1e-20)
        return p, o, inv

    cmp_end = lax.broadcasted_iota(jnp.int32, (1, nc), 1) * STRIDE_CMP + (L_CMP - 1)
    bias_c = jnp.where((cmp_end <= t_pos) & (cmp_end < seq), 0.0, NEG)
    has_cmp = tile4(t_pos >= L_CMP - 1)
    wlen = WINDOW + t
    ws = pl.multiple_of(jnp.maximum(s0 - WINDOW, 0), t)
    kp = ws + lax.broadcasted_iota(jnp.int32, (1, wlen), 1)
    bias_w = jnp.where((kp <= t_pos) & (kp > t_pos - WINDOW), 0.0, NEG)
    dist = jnp.right_shift(t_pos, SEL_SHIFT) - lane
    causal_blk = dist >= 0
    bonus = jnp.where((lane == 0) | (causal_blk & (dist < N_FORCED_LOCAL)), FORCE_BONUS, 0.0)
    brow = lax.broadcasted_iota(jnp.int32, (LANES, t), 0)

    def with_ones(v, g):
        in_g = (lane >= g * ATT_HEAD_DIM) & (lane < (g + 1) * ATT_HEAD_DIM)
        return v * jnp.where(in_g, 1.0, 0.0).astype(BF16) + jnp.where(in_g, 0.0, 1.0).astype(BF16)

    groups = range(ATT_KV_GROUPS)
    den = [(1 - g) * ATT_HEAD_DIM for g in groups]
    qgs, o_cs, lhss = [], [], []
    for g in groups:
        qg = jnp.concatenate(
            [qx_ref[:, (g * ATT_HPG + n) * LANES:(g * ATT_HPG + n + 1) * LANES]
             for n in range(ATT_HPG)], axis=0)
        qgs.append(qg)

        s_c = per_head(lambda a: a + bias_c, _nt(qg, kc_ref[...]))
        p_c, o_c, inv_c = softmax_pv(s_c, with_ones(vc_ref[...], g), den[g])
        inv_c = jnp.where(has_cmp, inv_c, 0.0)
        o_cs.append(o_c * inv_c)
        imp4 = _mm(p_c, ovl_ref[...]) * inv_c
        imp = imp4[0:t]
        for n in range(1, ATT_HPG):
            imp = imp + imp4[n * t:(n + 1) * t]

        imp_t = jnp.where(causal_blk, imp + bonus, -1.0).T
        sel_t = jnp.zeros((LANES, t), F32)
        for _ in range(N_SELECT):
            mx = jnp.max(imp_t, axis=0, keepdims=True)
            idx = jnp.min(jnp.where(imp_t == mx, brow, LANES), axis=0, keepdims=True)
            pick = brow == idx
            sel_t = jnp.where(pick, 1.0, sel_t)
            imp_t = jnp.where(pick, -3e38, imp_t)
        unsel = jnp.where(causal_blk, 1.0 - sel_t.T, 1.0).astype(BF16)

        lhss.append(jnp.concatenate([qg, tile4(unsel)], axis=1))

    lhs = jnp.concatenate(lhss, axis=0)
    m_ref[...] = jnp.full(m_ref.shape, NEG, F32)
    acc_ref[...] = jnp.zeros(acc_ref.shape, F32)

    def sel_tile(kt, diagonal):
        k0 = pl.multiple_of(kt * SEL_TK, SEL_TK)
        rhs = jnp.concatenate([kva_ref[pl.ds(k0, SEL_TK), 0:128],
                               ene_ref[pl.ds(k0, SEL_TK), :]], axis=1)
        v_t = kva_ref[pl.ds(k0, SEL_TK), 128:256]
        s = _nt(lhs, rhs)
        if diagonal:
            kpos = k0 + lax.broadcasted_iota(jnp.int32, (1, SEL_TK), 1)
            future = kpos > t_pos
            s = jnp.concatenate([jnp.where(future, NEG, s[n * t:(n + 1) * t])
                                 for n in range(ATT_HEADS)], axis=0)
        m_old = m_ref[...]
        m_new = jnp.maximum(m_old, jnp.max(s, axis=-1, keepdims=True))
        m_ref[...] = m_new
        p = jnp.concatenate([s[:, j * LANES:(j + 1) * LANES] - m_new
                             for j in range(SEL_TK // LANES)], axis=1)
        p = jnp.exp2(p).astype(BF16)
        pv = jnp.concatenate([_mm(p[g * rows:(g + 1) * rows], with_ones(v_t, g)) for g in groups],
                             axis=0)
        acc_ref[...] = jnp.exp2(m_old - m_new) * acc_ref[...] + pv

    n_tiles = (s0 + t + SEL_TK - 1) // SEL_TK

    def full_tile(kt, carry):
        sel_tile(kt, False)
        return carry

    lax.fori_loop(0, n_tiles - 1, full_tile, 0)
    sel_tile(n_tiles - 1, True)

    pieces = []
    for g in groups:
        acc = acc_ref[g * rows:(g + 1) * rows, :]
        o_s = acc * (1.0 / jnp.maximum(acc[:, den[g]:den[g] + 1], 1e-20))

        s_w = per_head(lambda a: a + bias_w, _nt(qgs[g], kva_ref[pl.ds(ws, wlen), 256:384]))
        _, o_w, inv_w = softmax_pv(s_w, with_ones(kva_ref[pl.ds(ws, wlen), 384:512], g), den[g])
        o_w = o_w * inv_w

        o_c = o_cs[g]
        for n in range(ATT_HPG):
            c0 = (g * ATT_HPG + n) * N_BRANCH
            r = slice(n * t, (n + 1) * t)
            ln = slice(g * ATT_HEAD_DIM, (g + 1) * ATT_HEAD_DIM)
            pieces.append(gates[:, c0:c0 + 1] * o_c[r, ln]
                          + gates[:, c0 + 1:c0 + 2] * o_s[r, ln]
                          + gates[:, c0 + 2:c0 + 3] * o_w[r, ln])
    out_ref[...] = jnp.concatenate(pieces, axis=1).astype(BF16)


def _nsa(qx, gate, kc, vc, kva, ovl, ene, batch, seq):
    n = qx.shape[0]
    nqb = seq // Q_BLOCK
    nc = kc.shape[1]
    row = lambda w: pl.BlockSpec((Q_BLOCK, w), lambda b, i: (b * nqb + i, 0))
    return pl.pallas_call(
        functools.partial(_nsa_kernel, seq=seq),
        grid=(batch, nqb),
        in_specs=[row(ATT_HEADS * LANES), row(LANES),
                  pl.BlockSpec((None, nc, LANES), lambda b, i: (b, 0, 0)),
                  pl.BlockSpec((None, nc, LANES), lambda b, i: (b, 0, 0)),
                  pl.BlockSpec((None, seq, 4 * LANES), lambda b, i: (b, 0, 0)),
                  pl.BlockSpec(ovl.shape, lambda b, i: (0, 0)),
                  pl.BlockSpec(ene.shape, lambda b, i: (0, 0))],
        out_specs=row(ATT_WIDTH),
        out_shape=jax.ShapeDtypeStruct((n, ATT_WIDTH), BF16),
        scratch_shapes=[pltpu.VMEM((ATT_HEADS * Q_BLOCK, LANES), F32),
                        pltpu.VMEM((ATT_HEADS * Q_BLOCK, LANES), F32)],
        compiler_params=_cparams(("arbitrary", "arbitrary")),
        name="nsa",
    )(qx, gate, kc, vc, kva.reshape(batch, seq, 4 * LANES), ovl, ene)


def _hgrn2_kernel(q_ref, f_ref, i_ref, g_ref, lb_ref, ng_ref, out_ref, st_ref, logf_ref, kk_ref):
    c = HG_CHUNK
    sub = HG_SUB
    nsub = c // sub

    @pl.when(pl.program_id(2) == 0)
    def _():
        st_ref[...] = jnp.zeros_like(st_ref)

    lb = lb_ref[...]
    f_hat = jax.nn.sigmoid(f_ref[...])
    logf_ref[...] = jnp.log(lb + (1.0 - lb) * f_hat)
    kk_ref[...] = (1.0 - lb) * (1.0 - f_hat)

    ri = lax.broadcasted_iota(jnp.int32, (c, c), 0)
    ci = lax.broadcasted_iota(jnp.int32, (c, c), 1)
    tril = jnp.where(ci <= ri, 1.0, 0.0).astype(BF16)
    row = lax.broadcasted_iota(jnp.int32, (c, 1), 0)
    row_sub = jnp.right_shift(row, SUB_SHIFT)
    row_loc = row - row_sub * sub
    ng = ng_ref[...]

    def chunk(ic, carry):
        r0 = pl.multiple_of(ic * c, c)
        q = q_ref[pl.ds(r0, c), :]
        k = kk_ref[pl.ds(r0, c), :]
        v = i_ref[pl.ds(r0, c), :].astype(BF16)
        lf_hi, lf_mid, lf_lo = _split3(logf_ref[pl.ds(r0, c), :])
        b = _mm(tril, lf_hi) + _mm(tril, lf_mid) + _mm(tril, lf_lo)
        st = st_ref[...]
        o = _nt((q * jnp.exp(b)).astype(BF16), st.astype(BF16))

        a_rows = [jnp.zeros((sub, c), F32)]
        for i in range(1, nsub):
            ref_b = b[i * sub:i * sub + 1, :]
            qs = q[i * sub:(i + 1) * sub, :] * jnp.exp(b[i * sub:(i + 1) * sub, :] - ref_b)
            ks = jnp.where(row < i * sub, k * jnp.exp(jnp.minimum(ref_b - b, 0.0)), 0.0)
            a_rows.append(_nt(qs.astype(BF16), ks.astype(BF16)))
        a = jnp.concatenate(a_rows, axis=0)

        b3 = b.reshape(nsub, sub, HG_DK)
        q3 = q.reshape(nsub, sub, HG_DK)
        k3 = k.reshape(nsub, sub, HG_DK)
        for s in range(sub):
            dec = jnp.exp(jnp.minimum(b3 - b3[:, s:s + 1, :], 0.0))
            w = (q3 * k3[:, s:s + 1, :] * dec).reshape(c, HG_DK)
            col = jnp.sum(w, axis=-1, keepdims=True)
            hit = (ci == row_sub * sub + s) & (row_loc >= s)
            a = a + jnp.where(hit, col, 0.0)

        o = o + _mm(a.astype(BF16), v)
        b_last = b[c - 1:c, :]
        kd = (k * jnp.exp(b_last - b)).astype(BF16)
        st_ref[...] = st * jnp.exp(b_last) + _tn(v, kd)

        o = o * lax.rsqrt(jnp.mean(o * o, axis=-1, keepdims=True) + RMS_EPS) * ng
        out_ref[pl.ds(r0, c), :] = (o * jax.nn.silu(g_ref[pl.ds(r0, c), :])).astype(BF16)
        return carry

    lax.fori_loop(0, q_ref.shape[0] // c, chunk, 0)


def _hgrn2(hg, lb, ng, batch, seq):
    n = hg.shape[0]
    tc = HG_TC
    nt = seq // tc
    col = lambda j: pl.BlockSpec((tc, HG_DK), lambda b, h, i, j=j: (b * nt + i, j * HG_HEADS + h))
    return pl.pallas_call(
        _hgrn2_kernel,
        grid=(batch, HG_HEADS, nt),
        in_specs=[col(0), col(1), col(2), col(3),
                  pl.BlockSpec((1, HG_DK), lambda b, h, i: (0, h)),
                  pl.BlockSpec((1, HG_DV), lambda b, h, i: (0, 0))],
        out_specs=pl.BlockSpec((tc, HG_DV), lambda b, h, i: (b * nt + i, h)),
        out_shape=jax.ShapeDtypeStruct((n, HG_WIDTH), BF16),
        scratch_shapes=[pltpu.VMEM((HG_DV, HG_DK), F32),
                        pltpu.VMEM((tc, HG_DK), F32),
                        pltpu.VMEM((tc, HG_DK), F32)],
        compiler_params=_cparams(("arbitrary", "arbitrary", "arbitrary")),
        name="hgrn2",
    )(hg, hg, hg, hg, lb, ng)


def _store_row_tiles(ref, v):
    rows = v.shape[0]
    for s in range(ROW_TILE):
        ref[pl.ds(s, rows, stride=ROW_TILE), :] = v[:, s * LANES:(s + 1) * LANES]


def _load_row_tiles(ref, rows):
    return jnp.concatenate([ref[pl.ds(s, rows, stride=ROW_TILE), :] for s in range(ROW_TILE)],
                           axis=1)


def _layer_norm(y, g, b):
    mu = jnp.mean(y, axis=-1, keepdims=True)
    d = y - mu
    var = jnp.mean(d * d, axis=-1, keepdims=True)
    return d * lax.rsqrt(var + LN_EPS) * g + b


def _out_proj_kernel(att_ref, rec_ref, x_ref, wo_ref, g_ref, b_ref, wr_ref, br_ref,
                     h_ref, ht_ref, ri_ref, rw_ref):
    mix = _mm(att_ref[...], wo_ref[0:ATT_WIDTH, :]) + _mm(rec_ref[...], wo_ref[ATT_WIDTH:, :])
    h = _layer_norm(DN_ALPHA * x_ref[...] + mix, g_ref[...], b_ref[...])
    h_ref[...] = h
    _store_row_tiles(ht_ref, h)

    hh, hm, hl = _split3(h)
    wh, wm, wl = wr_ref[0], wr_ref[1], wr_ref[2]
    lg = (_mm(hh, wh) + _mm(hh, wm) + _mm(hm, wh) + _mm(hh, wl) + _mm(hl, wh) + _mm(hm, wm)
          + br_ref[...])
    lane = lax.broadcasted_iota(jnp.int32, lg.shape, 1)
    ri = jnp.zeros(lg.shape, jnp.int32)
    rw = jnp.zeros(lg.shape, F32)
    v0 = None
    for r in range(TOP_K):
        mx = jnp.max(lg, axis=-1, keepdims=True)
        idx = jnp.min(jnp.where(lg == mx, lane, LANES), axis=-1, keepdims=True)
        if r == 0:
            v0 = mx
        ri = jnp.where(lane == r, idx, ri)
        rw = jnp.where(lane == r, jnp.exp(mx - v0), rw)
        lg = jnp.where(lane == idx, -3e38, lg)
    ri_ref[...] = ri
    rw_ref[...] = rw / jnp.sum(rw, axis=-1, keepdims=True)


def _out_proj(att, rec, x2, wo, g, b, wr3, br):
    n = x2.shape[0]
    tm = IN_TM
    row = lambda w: pl.BlockSpec((tm, w), lambda i: (i, 0))
    full = lambda a: pl.BlockSpec(a.shape, lambda i: (0,) * a.ndim)
    return pl.pallas_call(
        _out_proj_kernel,
        grid=(n // tm,),
        in_specs=[row(ATT_WIDTH), row(HG_WIDTH), row(D_MODEL), full(wo), full(g), full(b),
                  full(wr3), full(br)],
        out_specs=[row(D_MODEL), pl.BlockSpec((tm * ROW_TILE, LANES), lambda i: (i, 0)),
                   row(LANES), row(LANES)],
        out_shape=[jax.ShapeDtypeStruct((n, D_MODEL), F32),
                   jax.ShapeDtypeStruct((n * ROW_TILE, LANES), F32),
                   jax.ShapeDtypeStruct((n, LANES), jnp.int32),
                   jax.ShapeDtypeStruct((n, LANES), F32)],
        compiler_params=_cparams(("arbitrary",)),
        name="out_proj",
    )(att, rec, x2, wo, g, b, wr3, br)


def _route_kernel(ri_ref, rank_ref, cnt_ref, carry_ref):
    tm = ri_ref.shape[0]

    @pl.when(pl.program_id(0) == 0)
    def _():
        carry_ref[...] = jnp.zeros_like(carry_ref)

    ri = ri_ref[...]
    lane = lax.broadcasted_iota(jnp.int32, (tm, LANES), 1)
    hot = [lane == ri[:, k:k + 1] for k in range(TOP_K)]
    cnt = jnp.where(hot[0], 1.0, 0.0)
    for k in range(1, TOP_K):
        cnt = cnt + jnp.where(hot[k], 1.0, 0.0)
    tri = jnp.where(lax.broadcasted_iota(jnp.int32, (tm, tm), 1)
                    < lax.broadcasted_iota(jnp.int32, (tm, tm), 0), 1.0, 0.0).astype(BF16)
    before = carry_ref[...] + _mm(tri, cnt.astype(BF16))
    rank = jnp.zeros((tm, LANES), jnp.int32)
    for k in range(TOP_K):
        rk = jnp.sum(jnp.where(hot[k], before, 0.0), axis=-1, keepdims=True)
        rank = jnp.where(lane == k, rk.astype(jnp.int32), rank)
    rank_ref[...] = rank
    carry = carry_ref[...] + jnp.sum(cnt, axis=0, keepdims=True)
    carry_ref[...] = carry
    cnt_ref[...] = carry


def _route(ri):
    n = ri.shape[0]
    tm = ROUTE_TM
    return pl.pallas_call(
        _route_kernel,
        grid=(n // tm,),
        in_specs=[pl.BlockSpec((tm, LANES), lambda i: (i, 0))],
        out_specs=[pl.BlockSpec((tm, LANES), lambda i: (i, 0)),
                   pl.BlockSpec((1, LANES), lambda i: (0, 0))],
        out_shape=[jax.ShapeDtypeStruct((n, LANES), jnp.int32),
                   jax.ShapeDtypeStruct((1, LANES), F32)],
        scratch_shapes=[pltpu.VMEM((1, LANES), F32)],
        compiler_params=_cparams(("arbitrary",)),
        name="route",
    )(ri)


def _tile_rows(ref, row):
    return ref.at[pl.ds(pl.multiple_of(row * ROW_TILE, ROW_TILE), ROW_TILE), :]


def _dispatch_kernel(pos_ref, pad_ref, ht_ref, xs_hbm, zero_ref, sem, pad_sem):
    i = pl.program_id(0)
    tm = ht_ref.shape[0] // ROW_TILE

    def issue(r, carry):
        for k in range(TOP_K):
            p = pos_ref[(i * tm + r) * TOP_K + k]
            pltpu.make_async_copy(_tile_rows(ht_ref, r), _tile_rows(xs_hbm, p), sem).start()
        return carry

    lax.fori_loop(0, tm, issue, 0, unroll=GATHER_UNROLL // TOP_K)

    @pl.when(i == 0)
    def _():
        zero_ref[...] = jnp.zeros_like(zero_ref)
        n_span = (pad_ref.shape[0] - 1) // 2

        def per_span(e, carry):
            start = pad_ref[e]

            def fill(j, c):
                pltpu.make_async_copy(zero_ref, _tile_rows(xs_hbm, start + j), pad_sem).start()
                return c

            return lax.fori_loop(0, pad_ref[n_span + e], fill, carry)

        lax.fori_loop(0, n_span, per_span, 0)

        def drain(j, c):
            pltpu.make_async_copy(zero_ref, _tile_rows(xs_hbm, 0), pad_sem).wait()
            return c

        lax.fori_loop(0, pad_ref[2 * n_span], drain, 0)

    for k in range(TOP_K):
        pltpu.make_async_copy(ht_ref, xs_hbm.at[pl.ds(0, tm * ROW_TILE), :], sem).wait()


def _dispatch(pos, pad_tab, ht, n_rows_pad):
    n = ht.shape[0] // ROW_TILE
    tm = CMB_TM
    grid_spec = pltpu.PrefetchScalarGridSpec(
        num_scalar_prefetch=2,
        grid=(n // tm,),
        in_specs=[pl.BlockSpec((tm * ROW_TILE, LANES), lambda i, pos, pad: (i, 0))],
        out_specs=pl.BlockSpec(memory_space=pl.ANY),
        scratch_shapes=[pltpu.VMEM((ROW_TILE, LANES), F32), pltpu.SemaphoreType.DMA,
                        pltpu.SemaphoreType.DMA],
    )
    return pl.pallas_call(
        _dispatch_kernel,
        grid_spec=grid_spec,
        out_shape=jax.ShapeDtypeStruct((n_rows_pad * ROW_TILE, LANES), F32),
        compiler_params=_cparams(("arbitrary",)),
        name="dispatch",
    )(pos, pad_tab, ht)


def _experts_kernel(be_ref, nu_ref, xs_ref, wgu_ref, bgu_ref, wdn_ref, bdn_ref, y_ref):
    i = pl.program_id(0)
    bm = xs_ref.shape[0] // ROW_TILE

    @pl.when(i < nu_ref[0])
    def _():
        xb = _load_row_tiles(xs_ref, bm).astype(BF16)
        h = _mm(xb, wgu_ref[...]) + bgu_ref[...]
        gate = jnp.minimum(h[:, :D_FF], SWIGLU_LIMIT)
        up = jnp.clip(h[:, D_FF:], -SWIGLU_LIMIT, SWIGLU_LIMIT)
        act = gate * jax.nn.sigmoid(SWIGLU_ALPHA * gate) * (up + 1.0)
        _store_row_tiles(y_ref, _mm(act.astype(BF16), wdn_ref[...]) + bdn_ref[...])

    @pl.when(i >= nu_ref[0])
    def _():
        y_ref[...] = jnp.zeros_like(y_ref)


def _experts(blk_expert, n_used, xs, wgu, bgu, wdn, bdn):
    nblk = blk_expert.shape[0]
    bm = MOE_BM
    used = lambda i, nu: jnp.minimum(i, nu[0] - 1)
    grid_spec = pltpu.PrefetchScalarGridSpec(
        num_scalar_prefetch=2,
        grid=(nblk,),
        in_specs=[pl.BlockSpec((bm * ROW_TILE, LANES), lambda i, be, nu: (used(i, nu), 0)),
                  pl.BlockSpec((None, D_MODEL, 2 * D_FF), lambda i, be, nu: (be[used(i, nu)], 0, 0)),
                  pl.BlockSpec((None, 1, 2 * D_FF), lambda i, be, nu: (be[used(i, nu)], 0, 0)),
                  pl.BlockSpec((None, D_FF, D_MODEL), lambda i, be, nu: (be[used(i, nu)], 0, 0)),
                  pl.BlockSpec((None, 1, D_MODEL), lambda i, be, nu: (be[used(i, nu)], 0, 0))],
        out_specs=pl.BlockSpec((bm * ROW_TILE, LANES), lambda i, be, nu: (i, 0)),
    )
    return pl.pallas_call(
        _experts_kernel,
        grid_spec=grid_spec,
        out_shape=jax.ShapeDtypeStruct((nblk * bm * ROW_TILE, LANES), F32),
        compiler_params=_cparams(("arbitrary",)),
        name="experts",
    )(blk_expert, n_used, xs, wgu, bgu, wdn, bdn)


def _combine_kernel(pos_ref, y_hbm, h_ref, rw_ref, g_ref, b_ref, out_ref, ybuf, sem):
    i = pl.program_id(0)
    nstep = pl.num_programs(0)
    tm = ybuf.shape[2] // ROW_TILE

    def gather(step, slot):
        def issue(r, carry):
            for k in range(TOP_K):
                p = pos_ref[(step * tm + r) * TOP_K + k]
                pltpu.make_async_copy(_tile_rows(y_hbm, p), _tile_rows(ybuf.at[slot, k], r),
                                      sem.at[slot]).start()
            return carry
        lax.fori_loop(0, tm, issue, 0, unroll=GATHER_UNROLL // TOP_K)

    @pl.when(i == 0)
    def _():
        gather(0, 0)

    @pl.when(i + 1 < nstep)
    def _():
        gather(i + 1, (i + 1) % 2)

    slot = i % 2
    for k in range(TOP_K):
        pltpu.make_async_copy(y_hbm.at[pl.ds(0, tm * ROW_TILE), :], ybuf.at[slot, k],
                              sem.at[slot]).wait()

    rw = rw_ref[...]
    ffn = rw[:, 0:1] * _load_row_tiles(ybuf.at[slot, 0], tm)
    for k in range(1, TOP_K):
        ffn = ffn + rw[:, k:k + 1] * _load_row_tiles(ybuf.at[slot, k], tm)
    out_ref[...] = _layer_norm(DN_ALPHA * h_ref[...] + ffn, g_ref[...], b_ref[...])


def _combine(pos, ys, h1, rw, g, b):
    n = h1.shape[0]
    tm = CMB_TM
    grid_spec = pltpu.PrefetchScalarGridSpec(
        num_scalar_prefetch=1,
        grid=(n // tm,),
        in_specs=[pl.BlockSpec(memory_space=pl.ANY),
                  pl.BlockSpec((tm, D_MODEL), lambda i, pos: (i, 0)),
                  pl.BlockSpec((tm, LANES), lambda i, pos: (i, 0)),
                  pl.BlockSpec((1, D_MODEL), lambda i, pos: (0, 0)),
                  pl.BlockSpec((1, D_MODEL), lambda i, pos: (0, 0))],
        out_specs=pl.BlockSpec((tm, D_MODEL), lambda i, pos: (i, 0)),
        scratch_shapes=[pltpu.VMEM((2, TOP_K, tm * ROW_TILE, LANES), F32),
                        pltpu.SemaphoreType.DMA((2,))],
    )
    return pl.pallas_call(
        _combine_kernel,
        grid_spec=grid_spec,
        out_shape=jax.ShapeDtypeStruct((n, D_MODEL), F32),
        compiler_params=_cparams(("arbitrary",)),
        name="combine",
    )(pos, ys, h1, rw, g, b)


def _rope_tables(pos):
    inv = ROPE_THETA ** (-jnp.arange(0, ROPE_DIM, 2, dtype=F32) / ROPE_DIM)
    ang = pos.astype(F32)[..., None] * inv
    cos, sin = jnp.cos(ang), jnp.sin(ang)
    pad = jnp.zeros(ang.shape[:-1] + (ATT_HEAD_DIM - ROPE_DIM,), F32)
    c64 = jnp.concatenate([cos, cos, pad + 1.0], axis=-1)
    s1_64 = jnp.concatenate([-sin, jnp.zeros_like(sin), pad], axis=-1)
    s2_64 = jnp.concatenate([jnp.zeros_like(sin), sin, pad], axis=-1)
    dup = lambda a: jnp.concatenate([a, a], axis=-1)
    return dup(c64), dup(s1_64), dup(s2_64)


def _expand_cmp_weights(w1, w2, pe):
    g = ATT_KV_GROUPS
    eye = jnp.eye(g, dtype=F32)
    w1r = w1.reshape(L_CMP, ATT_HEAD_DIM, CMP_HIDDEN)
    w1x = jnp.einsum('ldh,ab->ladbh', w1r, eye).reshape(L_CMP, g * ATT_HEAD_DIM, g * CMP_HIDDEN)
    half = STRIDE_CMP
    top = w1x[:half].reshape(half * g * ATT_HEAD_DIM, g * CMP_HIDDEN).astype(BF16)
    bot = w1x[half:].reshape(half * g * ATT_HEAD_DIM, g * CMP_HIDDEN).astype(BF16)
    w2x = jnp.einsum('hd,ab->ahbd', w2, eye).reshape(g * CMP_HIDDEN, g * ATT_HEAD_DIM).astype(BF16)
    pex = jnp.broadcast_to(pe[:, None, :], (L_CMP, g, ATT_HEAD_DIM))
    pet = pex[:half].reshape(1, half * g * ATT_HEAD_DIM)
    peb = pex[half:].reshape(1, half * g * ATT_HEAD_DIM)
    return top, bot, w2x, pet, peb


def _overlap_matrix(nc_rows, seq):
    n_cmp = (seq - L_CMP) // STRIDE_CMP + 1
    cs = np.arange(nc_rows) * STRIDE_CMP
    ss = np.arange(LANES) * L_SEL
    ov = ((cs[:, None] < ss[None, :] + L_SEL) & (ss[None, :] < cs[:, None] + L_CMP)
          & (np.arange(nc_rows)[:, None] < n_cmp) & (ss[None, :] < seq))
    return jnp.asarray(ov.astype(np.float32), dtype=BF16)


def _block_mask_keys(seq):
    hot = (np.arange(seq)[:, None] // L_SEL) == np.arange(LANES)[None, :]
    return jnp.asarray(np.where(hot, NEG, 0.0).astype(np.float32), dtype=BF16)


def _route_tables(top_i, rank, counts, nblk, bm):
    experts = jnp.arange(N_EXPERTS, dtype=jnp.int32)
    padded = (counts + bm - 1) // bm * bm
    pends = jnp.cumsum(padded)
    pstarts = pends - padded
    onehot = top_i[:, :, None] == experts[None, None, :]
    pos = rank + jnp.sum(jnp.where(onehot, pstarts[None, None, :], 0), axis=-1)
    blk_start = jnp.arange(nblk, dtype=jnp.int32) * bm
    blk_expert = jnp.minimum(jnp.sum((pends[None, :] <= blk_start[:, None]).astype(jnp.int32), axis=1),
                             N_EXPERTS - 1)
    n_used = (pends[-1] // bm).reshape(1)
    n_rows_pad = nblk * bm
    starts = jnp.concatenate([pstarts + counts, pends[-1:]])
    lens = jnp.concatenate([padded - counts, n_rows_pad - pends[-1:]])
    pad_tab = jnp.concatenate([starts, lens, jnp.sum(lens).reshape(1)])
    return pos.reshape(-1).astype(jnp.int32), blk_expert, n_used.astype(jnp.int32), pad_tab.astype(jnp.int32)


def kernel(x, positions, w_in, pe_cmp, w_ck1, w_ck2, w_cv1, w_cv2, hg_lb, hg_norm_g, w_o, ln1_g,
           ln1_b, w_router, b_router, w_gate_up, b_gate_up, w_down, b_down, ln2_g, ln2_b):
    batch, seq, d = x.shape
    n = batch * seq
    assert d == D_MODEL and seq % SEL_TK == 0 and seq % HG_TC == 0 and n % IN_TM == 0
    assert seq // L_SEL <= LANES and seq >= WINDOW + Q_BLOCK
    l = 0
    x2 = x.reshape(n, d)

    n_att = ATT_WIDTH + 6 * KV_WIDTH
    w_pad = jnp.concatenate(
        [w_in[l][:, :n_att + N_GATE], jnp.zeros((d, LANES - N_GATE), F32), w_in[l][:, n_att + N_GATE:]],
        axis=1).astype(BF16)
    wkt, wkb, wk2, pet, peb = _expand_cmp_weights(w_ck1[l], w_ck2[l], pe_cmp[l])
    wvt, wvb, wv2, _, _ = _expand_cmp_weights(w_cv1[l], w_cv2[l], pe_cmp[l])
    lbs = jnp.cumsum(jax.nn.softmax(hg_lb.astype(F32), axis=0), axis=0)[l].reshape(1, HG_HEADS * HG_DK)
    wr_pad = jnp.concatenate([w_router[l], jnp.zeros((d, LANES - N_EXPERTS), F32)], axis=1)
    wr3 = jnp.stack(_split3(wr_pad))
    br_pad = jnp.concatenate([b_router[l], jnp.full((LANES - N_EXPERTS,), NEG, F32)]).reshape(1, LANES)

    c_tab, s1_tab, s2_tab = _rope_tables(positions.reshape(n))
    nc_rows = seq // STRIDE_CMP
    cmp_end = jnp.minimum(jnp.arange(nc_rows) * STRIDE_CMP + L_CMP - 1, seq - 1)
    cc, s1c, s2c = _rope_tables(positions[:, cmp_end])

    qx, kcmp, vcmp, kva, gate, hg = _in_proj(x2, w_pad, c_tab, s1_tab, s2_tab)
    kc, vc = _compress(kcmp.reshape(batch, nc_rows, STRIDE_CMP * LANES),
                       vcmp.reshape(batch, nc_rows, STRIDE_CMP * LANES),
                       pet, peb, wkt, wkb, wvt, wvb, wk2, wv2, cc, s1c, s2c)
    att = _nsa(qx, gate, kc, vc, kva, _overlap_matrix(nc_rows, seq), _block_mask_keys(seq),
               batch, seq)
    rec = _hgrn2(hg, lbs, hg_norm_g[l].reshape(1, HG_DV), batch, seq)
    h1, h1t, ri, rw = _out_proj(att, rec, x2, w_o[l].astype(BF16), ln1_g[l].reshape(1, d),
                                ln1_b[l].reshape(1, d), wr3, br_pad)

    bm = MOE_BM
    a = n * TOP_K
    nblk = -(-(a + N_EXPERTS * (bm - 1)) // bm)
    rank, counts = _route(ri)
    pos, blk_expert, n_used, pad_tab = _route_tables(
        ri[:, :TOP_K], rank[:, :TOP_K], counts[0, :N_EXPERTS].astype(jnp.int32), nblk, bm)
    xs = _dispatch(pos, pad_tab, h1t, nblk * bm)
    ys = _experts(blk_expert, n_used, xs, w_gate_up[l].astype(BF16),
                  b_gate_up[l].reshape(N_EXPERTS, 1, 2 * D_FF), w_down[l].astype(BF16),
                  b_down[l].reshape(N_EXPERTS, 1, d))
    out = _combine(pos, ys, h1, rw, ln2_g[l].reshape(1, d), ln2_b[l].reshape(1, d))
    return out.reshape(batch, seq, d)
```

```python
import functools

import numpy as np
import jax
import jax.numpy as jnp
from jax import lax
from jax.experimental import pallas as pl
from jax.experimental.pallas import tpu as pltpu

F32 = jnp.float32
BF16 = jnp.bfloat16

D_MODEL = 1024
ATT_HEADS = 8
ATT_HEAD_DIM = 64
ATT_KV_GROUPS = 2
ATT_HPG = ATT_HEADS // ATT_KV_GROUPS
ATT_WIDTH = ATT_HEADS * ATT_HEAD_DIM
KV_WIDTH = ATT_KV_GROUPS * ATT_HEAD_DIM
N_BRANCH = 3
L_CMP = 32
STRIDE_CMP = 16
CMP_HIDDEN = 256
L_SEL = 64
N_SELECT = 16
N_FORCED_LOCAL = 2
FORCE_BONUS = 1000.0
WINDOW = 512
Q_BLOCK = 128
ROPE_THETA = 500000.0
ROPE_DIM = ATT_HEAD_DIM // 4
ROPE_HALF = ROPE_DIM // 2
HG_HEADS = 4
HG_DK = 128
HG_DV = 128
HG_WIDTH = HG_HEADS * HG_DV
HG_CHUNK = 64
HG_SUB = 16
HG_SAFE_DECAY = 60.0
N_EXPERTS = 32
TOP_K = 4
D_FF = 1024
SWIGLU_LIMIT = 7.0
SWIGLU_ALPHA = 1.702
DEPTH = 1
DN_ALPHA = (2 * DEPTH) ** 0.25
LN_EPS = 1e-5
RMS_EPS = 1e-6
NEG = -1e30
LOG2E = 1.4426950408889634
SEL_SHIFT = L_SEL.bit_length() - 1
SUB_SHIFT = HG_SUB.bit_length() - 1

LANES = 128
VMEM_LIMIT = 56 * 1024 * 1024
IN_TM = 512
SEL_TK = 512
HG_TC = 512
MOE_BM = 256
CMB_TM = 256
GATHER_UNROLL = 8
ROW_TILE = D_MODEL // LANES
ROUTE_TM = 512

C_Q = 0
C_KCMP = 512
C_VCMP = 640
C_KVA = 768
C_GATE = 1280
C_HG = 1408
IN_COLS = C_HG + 4 * HG_WIDTH
N_GATE = ATT_HEADS * N_BRANCH


def _nt(a, b):
    return lax.dot_general(a, b, (((1,), (1,)), ((), ())), preferred_element_type=F32)


def _tn(a, b):
    return lax.dot_general(a, b, (((0,), (0,)), ((), ())), preferred_element_type=F32)


def _mm(a, b):
    return jnp.dot(a, b, preferred_element_type=F32)


def _split3(x):
    hi = x.astype(BF16)
    r1 = x - hi.astype(F32)
    mid = r1.astype(BF16)
    lo = (r1 - mid.astype(F32)).astype(BF16)
    return hi, mid, lo


def _rope(v, c, s1, s2):
    return v * c + pltpu.roll(v, LANES - ROPE_HALF, 1) * s1 + pltpu.roll(v, ROPE_HALF, 1) * s2


def _cparams(sem):
    return pltpu.CompilerParams(dimension_semantics=sem, vmem_limit_bytes=VMEM_LIMIT)


def _in_proj_kernel(x_ref, w_ref, c_ref, s1_ref, s2_ref,
                    qx_ref, kcmp_ref, vcmp_ref, kva_ref, gate_ref, hg_ref):
    xb = x_ref[...].astype(BF16)
    c, s1, s2 = c_ref[...], s1_ref[...], s2_ref[...]
    lane = lax.broadcasted_iota(jnp.int32, (1, LANES), 1)
    low = lane < ATT_HEAD_DIM

    q = _mm(xb, w_ref[:, C_Q:C_Q + ATT_WIDTH])
    scale = ATT_HEAD_DIM ** -0.5 * LOG2E
    for j in range(ATT_WIDTH // LANES):
        v = _rope(q[:, j * LANES:(j + 1) * LANES], c, s1, s2) * scale
        vr = pltpu.roll(v, ATT_HEAD_DIM, 1)
        g = (2 * j) // ATT_HPG
        keep = low if g == 0 else jnp.logical_not(low)
        h0 = jnp.where(keep, v if g == 0 else vr, 0.0)
        h1 = jnp.where(keep, vr if g == 0 else v, 0.0)
        qx_ref[:, (2 * j) * LANES:(2 * j + 1) * LANES] = h0.astype(BF16)
        qx_ref[:, (2 * j + 1) * LANES:(2 * j + 2) * LANES] = h1.astype(BF16)

    kv = _mm(xb, w_ref[:, C_KCMP:C_GATE])
    kcmp_ref[...] = kv[:, 0:128]
    vcmp_ref[...] = kv[:, 128:256]
    kva_ref[:, 0:128] = _rope(kv[:, 256:384], c, s1, s2).astype(BF16)
    kva_ref[:, 128:256] = kv[:, 384:512].astype(BF16)
    kva_ref[:, 256:384] = _rope(kv[:, 512:640], c, s1, s2).astype(BF16)
    kva_ref[:, 384:512] = kv[:, 640:768].astype(BF16)

    gate_ref[...] = jax.nn.sigmoid(_mm(xb, w_ref[:, C_GATE:C_HG]))
    for j in range(4):
        hg_ref[:, j * HG_WIDTH:(j + 1) * HG_WIDTH] = _mm(
            xb, w_ref[:, C_HG + j * HG_WIDTH:C_HG + (j + 1) * HG_WIDTH])


def _in_proj(x2, w_pad, c_tab, s1_tab, s2_tab):
    n = x2.shape[0]
    tm = IN_TM
    row = lambda w: pl.BlockSpec((tm, w), lambda i: (i, 0))
    return pl.pallas_call(
        _in_proj_kernel,
        grid=(n // tm,),
        in_specs=[row(D_MODEL), pl.BlockSpec((D_MODEL, IN_COLS), lambda i: (0, 0)),
                  row(LANES), row(LANES), row(LANES)],
        out_specs=[row(ATT_HEADS * LANES), row(LANES), row(LANES), row(4 * LANES), row(LANES),
                   row(4 * HG_WIDTH)],
        out_shape=[jax.ShapeDtypeStruct((n, ATT_HEADS * LANES), BF16),
                   jax.ShapeDtypeStruct((n, LANES), F32),
                   jax.ShapeDtypeStruct((n, LANES), F32),
                   jax.ShapeDtypeStruct((n, 4 * LANES), BF16),
                   jax.ShapeDtypeStruct((n, LANES), F32),
                   jax.ShapeDtypeStruct((n, 4 * HG_WIDTH), F32)],
        compiler_params=_cparams(("arbitrary",)),
        name="in_proj",
    )(x2, w_pad, c_tab, s1_tab, s2_tab)


def _compress_kernel(kf_ref, vf_ref, pet_ref, peb_ref, wkt_ref, wkb_ref, wvt_ref, wvb_ref,
                     wk2_ref, wv2_ref, c_ref, s1_ref, s2_ref, kc_ref, vc_ref):
    nc = kf_ref.shape[0]

    def mlp(x, wt, wb, w2):
        top = (x + pet_ref[...]).astype(BF16)
        bot = (x + peb_ref[...]).astype(BF16)
        u = _mm(top, wt[...])
        v = _mm(bot, wb[...])
        h = u + pltpu.roll(v, nc - 1, 0)
        return _mm(jax.nn.gelu(h).astype(BF16), w2[...])

    kc = mlp(kf_ref[...], wkt_ref, wkb_ref, wk2_ref)
    kc_ref[...] = _rope(kc, c_ref[...], s1_ref[...], s2_ref[...]).astype(BF16)
    vc_ref[...] = mlp(vf_ref[...], wvt_ref, wvb_ref, wv2_ref).astype(BF16)


def _compress(kf, vf, pet, peb, wkt, wkb, wvt, wvb, wk2, wv2, cc, s1c, s2c):
    b, nc, w = kf.shape
    per_b = lambda width: pl.BlockSpec((None, nc, width), lambda i: (i, 0, 0))
    full = lambda a: pl.BlockSpec(a.shape, lambda i: (0,) * a.ndim)
    return pl.pallas_call(
        _compress_kernel,
        grid=(b,),
        in_specs=[per_b(w), per_b(w), full(pet), full(peb), full(wkt), full(wkb), full(wvt),
                  full(wvb), full(wk2), full(wv2), per_b(LANES), per_b(LANES), per_b(LANES)],
        out_specs=[per_b(LANES), per_b(LANES)],
        out_shape=[jax.ShapeDtypeStruct((b, nc, LANES), BF16)] * 2,
        compiler_params=_cparams(("arbitrary",)),
        name="compress",
    )(kf, vf, pet, peb, wkt, wkb, wvt, wvb, wk2, wv2, cc, s1c, s2c)


def _nsa_kernel(qx_ref, gate_ref, kc_ref, vc_ref, kva_ref, ovl_ref, ene_ref, out_ref,
                m_ref, acc_ref, *, seq):
    t = Q_BLOCK
    rows = ATT_HPG * t
    qb = pl.program_id(1)
    s0 = qb * t
    t_pos = s0 + lax.broadcasted_iota(jnp.int32, (t, 1), 0)
    nc = kc_ref.shape[0]
    gates = gate_ref[...]
    lane = lax.broadcasted_iota(jnp.int32, (1, LANES), 1)

    def tile4(a):
        return jnp.concatenate([a] * ATT_HPG, axis=0)

    def per_head(fn, s):
        return jnp.concatenate([fn(s[n * t:(n + 1) * t]) for n in range(ATT_HPG)], axis=0)

    def softmax_pv(s, vx, den_lane):
        m = jnp.max(s, axis=-1, keepdims=True)
        p = jnp.exp2(s - m).astype(BF16)
        o = _mm(p, vx)
        inv = 1.0 / jnp.maximum(o[:, den_lane:den_lane + 1], 1e-20)
        return p, o, inv

    cmp_end = lax.broadcasted_iota(jnp.int32, (1, nc), 1) * STRIDE_CMP + (L_CMP - 1)
    bias_c = jnp.where((cmp_end <= t_pos) & (cmp_end < seq), 0.0, NEG)
    has_cmp = tile4(t_pos >= L_CMP - 1)
    wlen = WINDOW + t
    ws = pl.multiple_of(jnp.maximum(s0 - WINDOW, 0), t)
    kp = ws + lax.broadcasted_iota(jnp.int32, (1, wlen), 1)
    bias_w = jnp.where((kp <= t_pos) & (kp > t_pos - WINDOW), 0.0, NEG)
    dist = jnp.right_shift(t_pos, SEL_SHIFT) - lane
    causal_blk = dist >= 0
    bonus = jnp.where((lane == 0) | (causal_blk & (dist < N_FORCED_LOCAL)), FORCE_BONUS, 0.0)
    brow = lax.broadcasted_iota(jnp.int32, (LANES, t), 0)

    def with_ones(v, g):
        in_g = (lane >= g * ATT_HEAD_DIM) & (lane < (g + 1) * ATT_HEAD_DIM)
        return v * jnp.where(in_g, 1.0, 0.0).astype(BF16) + jnp.where(in_g, 0.0, 1.0).astype(BF16)

    groups = range(ATT_KV_GROUPS)
    den = [(1 - g) * ATT_HEAD_DIM for g in groups]
    qgs, o_cs, lhss = [], [], []
    for g in groups:
        qg = jnp.concatenate(
            [qx_ref[:, (g * ATT_HPG + n) * LANES:(g * ATT_HPG + n + 1) * LANES]
             for n in range(ATT_HPG)], axis=0)
        qgs.append(qg)

        s_c = per_head(lambda a: a + bias_c, _nt(qg, kc_ref[...]))
        p_c, o_c, inv_c = softmax_pv(s_c, with_ones(vc_ref[...], g), den[g])
        inv_c = jnp.where(has_cmp, inv_c, 0.0)
        o_cs.append(o_c * inv_c)
        imp4 = _mm(p_c, ovl_ref[...]) * inv_c
        imp = imp4[0:t]
        for n in range(1, ATT_HPG):
            imp = imp + imp4[n * t:(n + 1) * t]

        imp_t = jnp.where(causal_blk, imp + bonus, -1.0).T
        sel_t = jnp.zeros((LANES, t), F32)
        for _ in range(N_SELECT):
            mx = jnp.max(imp_t, axis=0, keepdims=True)
            idx = jnp.min(jnp.where(imp_t == mx, brow, LANES), axis=0, keepdims=True)
            pick = brow == idx
            sel_t = jnp.where(pick, 1.0, sel_t)
            imp_t = jnp.where(pick, -3e38, imp_t)
        unsel = jnp.where(causal_blk, 1.0 - sel_t.T, 1.0).astype(BF16)

        lhss.append(jnp.concatenate([qg, tile4(unsel)], axis=1))

    lhs = jnp.concatenate(lhss, axis=0)
    m_ref[...] = jnp.full(m_ref.shape, NEG, F32)
    acc_ref[...] = jnp.zeros(acc_ref.shape, F32)

    def sel_tile(kt, diagonal):
        k0 = pl.multiple_of(kt * SEL_TK, SEL_TK)
        rhs = jnp.concatenate([kva_ref[pl.ds(k0, SEL_TK), 0:128],
                               ene_ref[pl.ds(k0, SEL_TK), :]], axis=1)
        v_t = kva_ref[pl.ds(k0, SEL_TK), 128:256]
        s = _nt(lhs, rhs)
        if diagonal:
            kpos = k0 + lax.broadcasted_iota(jnp.int32, (1, SEL_TK), 1)
            future = kpos > t_pos
            s = jnp.concatenate([jnp.where(future, NEG, s[n * t:(n + 1) * t])
                                 for n in range(ATT_HEADS)], axis=0)
        m_old = m_ref[...]
        m_new = jnp.maximum(m_old, jnp.max(s, axis=-1, keepdims=True))
        m_ref[...] = m_new
        p = jnp.concatenate([s[:, j * LANES:(j + 1) * LANES] - m_new
                             for j in range(SEL_TK // LANES)], axis=1)
        p = jnp.exp2(p).astype(BF16)
        pv = jnp.concatenate([_mm(p[g * rows:(g + 1) * rows], with_ones(v_t, g)) for g in groups],
                             axis=0)
        acc_ref[...] = jnp.exp2(m_old - m_new) * acc_ref[...] + pv

    n_tiles = (s0 + t + SEL_TK - 1) // SEL_TK

    def full_tile(kt, carry):
        sel_tile(kt, False)
        return carry

    lax.fori_loop(0, n_tiles - 1, full_tile, 0)
    sel_tile(n_tiles - 1, True)

    pieces = []
    for g in groups:
        acc = acc_ref[g * rows:(g + 1) * rows, :]
        o_s = acc * (1.0 / jnp.maximum(acc[:, den[g]:den[g] + 1], 1e-20))

        s_w = per_head(lambda a: a + bias_w, _nt(qgs[g], kva_ref[pl.ds(ws, wlen), 256:384]))
        _, o_w, inv_w = softmax_pv(s_w, with_ones(kva_ref[pl.ds(ws, wlen), 384:512], g), den[g])
        o_w = o_w * inv_w

        o_c = o_cs[g]
        for n in range(ATT_HPG):
            c0 = (g * ATT_HPG + n) * N_BRANCH
            r = slice(n * t, (n + 1) * t)
            ln = slice(g * ATT_HEAD_DIM, (g + 1) * ATT_HEAD_DIM)
            pieces.append(gates[:, c0:c0 + 1] * o_c[r, ln]
                          + gates[:, c0 + 1:c0 + 2] * o_s[r, ln]
                          + gates[:, c0 + 2:c0 + 3] * o_w[r, ln])
    out_ref[...] = jnp.concatenate(pieces, axis=1).astype(BF16)


def _nsa(qx, gate, kc, vc, kva, ovl, ene, batch, seq):
    n = qx.shape[0]
    nqb = seq // Q_BLOCK
    nc = kc.shape[1]
    row = lambda w: pl.BlockSpec((Q_BLOCK, w), lambda b, i: (b * nqb + i, 0))
    return pl.pallas_call(
        functools.partial(_nsa_kernel, seq=seq),
        grid=(batch, nqb),
        in_specs=[row(ATT_HEADS * LANES), row(LANES),
                  pl.BlockSpec((None, nc, LANES), lambda b, i: (b, 0, 0)),
                  pl.BlockSpec((None, nc, LANES), lambda b, i: (b, 0, 0)),
                  pl.BlockSpec((None, seq, 4 * LANES), lambda b, i: (b, 0, 0)),
                  pl.BlockSpec(ovl.shape, lambda b, i: (0, 0)),
                  pl.BlockSpec(ene.shape, lambda b, i: (0, 0))],
        out_specs=row(ATT_WIDTH),
        out_shape=jax.ShapeDtypeStruct((n, ATT_WIDTH), BF16),
        scratch_shapes=[pltpu.VMEM((ATT_HEADS * Q_BLOCK, LANES), F32),
                        pltpu.VMEM((ATT_HEADS * Q_BLOCK, LANES), F32)],
        compiler_params=_cparams(("arbitrary", "arbitrary")),
        name="nsa",
    )(qx, gate, kc, vc, kva.reshape(batch, seq, 4 * LANES), ovl, ene)


def _bmm(a, b):
    return lax.dot_general(a, b, (((2,), (1,)), ((0,), (0,))), preferred_element_type=F32)


def _bnt(a, b):
    return lax.dot_general(a, b, (((2,), (2,)), ((0,), (0,))), preferred_element_type=F32)


def _hgrn2_kernel(q_ref, f_ref, i_ref, g_ref, lb_ref, ng_ref, out_ref, st_ref, b_ref, kk_ref):
    c = HG_CHUNK
    sub = HG_SUB
    nsub = c // sub
    tc = q_ref.shape[0]
    nch = tc // c

    @pl.when(pl.program_id(2) == 0)
    def _():
        st_ref[...] = jnp.zeros_like(st_ref)

    lb = lb_ref[...]
    f_hat = jax.nn.sigmoid(f_ref[...])
    logf = jnp.log(lb + (1.0 - lb) * f_hat)
    kk_ref[...] = (1.0 - lb) * (1.0 - f_hat)

    ri = lax.broadcasted_iota(jnp.int32, (c, c), 0)
    ci = lax.broadcasted_iota(jnp.int32, (c, c), 1)
    lower = ci <= ri
    ng = ng_ref[...]

    tril3 = jnp.broadcast_to(jnp.where(lower, 1.0, 0.0).astype(BF16)[None], (nch, c, c))
    b3 = None
    for part in _split3(logf):
        term = _bmm(tril3, part.reshape(nch, c, HG_DK))
        b3 = term if b3 is None else b3 + term
    b_ref[...] = b3.reshape(tc, HG_DK)
    worst_decay = jnp.max(-b3[:, c - 1:c, :])

    def finish(o, rows):
        o = o * lax.rsqrt(jnp.mean(o * o, axis=-1, keepdims=True) + RMS_EPS) * ng
        out_ref[rows, :] = (o * jax.nn.silu(g_ref[rows, :])).astype(BF16)

    def whole_chunk_decay():
        b = b_ref[...].reshape(nch, c, HG_DK)
        eb = jnp.exp(b)
        qe = (q_ref[...].reshape(nch, c, HG_DK) * eb).astype(BF16)
        ke = kk_ref[...].reshape(nch, c, HG_DK) * jnp.exp(-b)
        v = i_ref[...].astype(BF16).reshape(nch, c, HG_DV)
        a = jnp.where(lower[None], _bnt(qe, ke.astype(BF16)), 0.0).astype(BF16)
        o_intra = _bmm(a, v)
        eb_last = eb[:, c - 1:c, :]
        kd = (ke * eb_last).astype(BF16)
        st = st_ref[...]
        outs = []
        for ic in range(nch):
            outs.append(o_intra[ic] + _nt(qe[ic], st.astype(BF16)))
            st = st * eb_last[ic] + _tn(v[ic], kd[ic])
        st_ref[...] = st
        finish(jnp.concatenate(outs, axis=0), slice(None))

    def sub_block_decay():
        lax.fori_loop(0, nch, chunk, 0)

    row = lax.broadcasted_iota(jnp.int32, (c, 1), 0)
    row_sub = jnp.right_shift(row, SUB_SHIFT)
    row_loc = row - row_sub * sub

    def chunk(ic, carry):
        r0 = pl.multiple_of(ic * c, c)
        q = q_ref[pl.ds(r0, c), :]
        k = kk_ref[pl.ds(r0, c), :]
        v = i_ref[pl.ds(r0, c), :].astype(BF16)
        b = b_ref[pl.ds(r0, c), :]
        st = st_ref[...]
        o = _nt((q * jnp.exp(b)).astype(BF16), st.astype(BF16))

        a_rows = [jnp.zeros((sub, c), F32)]
        for i in range(1, nsub):
            ref_b = b[i * sub:i * sub + 1, :]
            qs = q[i * sub:(i + 1) * sub, :] * jnp.exp(b[i * sub:(i + 1) * sub, :] - ref_b)
            ks = jnp.where(row < i * sub, k * jnp.exp(jnp.minimum(ref_b - b, 0.0)), 0.0)
            a_rows.append(_nt(qs.astype(BF16), ks.astype(BF16)))
        a = jnp.concatenate(a_rows, axis=0)

        b3 = b.reshape(nsub, sub, HG_DK)
        q3 = q.reshape(nsub, sub, HG_DK)
        k3 = k.reshape(nsub, sub, HG_DK)
        for s in range(sub):
            dec = jnp.exp(jnp.minimum(b3 - b3[:, s:s + 1, :], 0.0))
            w = (q3 * k3[:, s:s + 1, :] * dec).reshape(c, HG_DK)
            col = jnp.sum(w, axis=-1, keepdims=True)
            hit = (ci == row_sub * sub + s) & (row_loc >= s)
            a = a + jnp.where(hit, col, 0.0)

        o = o + _mm(a.astype(BF16), v)
        b_last = b[c - 1:c, :]
        kd = (k * jnp.exp(b_last - b)).astype(BF16)
        st_ref[...] = st * jnp.exp(b_last) + _tn(v, kd)
        finish(o, pl.ds(r0, c))
        return carry

    lax.cond(worst_decay < HG_SAFE_DECAY, whole_chunk_decay, sub_block_decay)


def _hgrn2(hg, lb, ng, batch, seq):
    n = hg.shape[0]
    tc = HG_TC
    nt = seq // tc
    col = lambda j: pl.BlockSpec((tc, HG_DK), lambda b, h, i, j=j: (b * nt + i, j * HG_HEADS + h))
    return pl.pallas_call(
        _hgrn2_kernel,
        grid=(batch, HG_HEADS, nt),
        in_specs=[col(0), col(1), col(2), col(3),
                  pl.BlockSpec((1, HG_DK), lambda b, h, i: (0, h)),
                  pl.BlockSpec((1, HG_DV), lambda b, h, i: (0, 0))],
        out_specs=pl.BlockSpec((tc, HG_DV), lambda b, h, i: (b * nt + i, h)),
        out_shape=jax.ShapeDtypeStruct((n, HG_WIDTH), BF16),
        scratch_shapes=[pltpu.VMEM((HG_DV, HG_DK), F32),
                        pltpu.VMEM((tc, HG_DK), F32),
                        pltpu.VMEM((tc, HG_DK), F32)],
        compiler_params=_cparams(("arbitrary", "arbitrary", "arbitrary")),
        name="hgrn2",
    )(hg, hg, hg, hg, lb, ng)


def _store_row_tiles(ref, v):
    rows = v.shape[0]
    for s in range(ROW_TILE):
        ref[pl.ds(s, rows, stride=ROW_TILE), :] = v[:, s * LANES:(s + 1) * LANES]


def _load_row_tiles(ref, rows):
    return jnp.concatenate([ref[pl.ds(s, rows, stride=ROW_TILE), :] for s in range(ROW_TILE)],
                           axis=1)


def _layer_norm(y, g, b):
    mu = jnp.mean(y, axis=-1, keepdims=True)
    d = y - mu
    var = jnp.mean(d * d, axis=-1, keepdims=True)
    return d * lax.rsqrt(var + LN_EPS) * g + b


def _out_proj_kernel(att_ref, rec_ref, x_ref, wo_ref, g_ref, b_ref, wr_ref, br_ref,
                     h_ref, ht_ref, ri_ref, rw_ref):
    mix = _mm(att_ref[...], wo_ref[0:ATT_WIDTH, :]) + _mm(rec_ref[...], wo_ref[ATT_WIDTH:, :])
    h = _layer_norm(DN_ALPHA * x_ref[...] + mix, g_ref[...], b_ref[...])
    h_ref[...] = h
    _store_row_tiles(ht_ref, h)

    hh, hm, hl = _split3(h)
    wh, wm, wl = wr_ref[0], wr_ref[1], wr_ref[2]
    lg = (_mm(hh, wh) + _mm(hh, wm) + _mm(hm, wh) + _mm(hh, wl) + _mm(hl, wh) + _mm(hm, wm)
          + br_ref[...])
    lane = lax.broadcasted_iota(jnp.int32, lg.shape, 1)
    ri = jnp.zeros(lg.shape, jnp.int32)
    rw = jnp.zeros(lg.shape, F32)
    v0 = None
    for r in range(TOP_K):
        mx = jnp.max(lg, axis=-1, keepdims=True)
        idx = jnp.min(jnp.where(lg == mx, lane, LANES), axis=-1, keepdims=True)
        if r == 0:
            v0 = mx
        ri = jnp.where(lane == r, idx, ri)
        rw = jnp.where(lane == r, jnp.exp(mx - v0), rw)
        lg = jnp.where(lane == idx, -3e38, lg)
    ri_ref[...] = ri
    rw_ref[...] = rw / jnp.sum(rw, axis=-1, keepdims=True)


def _out_proj(att, rec, x2, wo, g, b, wr3, br):
    n = x2.shape[0]
    tm = IN_TM
    row = lambda w: pl.BlockSpec((tm, w), lambda i: (i, 0))
    full = lambda a: pl.BlockSpec(a.shape, lambda i: (0,) * a.ndim)
    return pl.pallas_call(
        _out_proj_kernel,
        grid=(n // tm,),
        in_specs=[row(ATT_WIDTH), row(HG_WIDTH), row(D_MODEL), full(wo), full(g), full(b),
                  full(wr3), full(br)],
        out_specs=[row(D_MODEL), pl.BlockSpec((tm * ROW_TILE, LANES), lambda i: (i, 0)),
                   row(LANES), row(LANES)],
        out_shape=[jax.ShapeDtypeStruct((n, D_MODEL), F32),
                   jax.ShapeDtypeStruct((n * ROW_TILE, LANES), F32),
                   jax.ShapeDtypeStruct((n, LANES), jnp.int32),
                   jax.ShapeDtypeStruct((n, LANES), F32)],
        compiler_params=_cparams(("arbitrary",)),
        name="out_proj",
    )(att, rec, x2, wo, g, b, wr3, br)


def _route_kernel(ri_ref, rank_ref, cnt_ref, carry_ref):
    tm = ri_ref.shape[0]

    @pl.when(pl.program_id(0) == 0)
    def _():
        carry_ref[...] = jnp.zeros_like(carry_ref)

    ri = ri_ref[...]
    lane = lax.broadcasted_iota(jnp.int32, (tm, LANES), 1)
    hot = [lane == ri[:, k:k + 1] for k in range(TOP_K)]
    cnt = jnp.where(hot[0], 1.0, 0.0)
    for k in range(1, TOP_K):
        cnt = cnt + jnp.where(hot[k], 1.0, 0.0)
    tri = jnp.where(lax.broadcasted_iota(jnp.int32, (tm, tm), 1)
                    < lax.broadcasted_iota(jnp.int32, (tm, tm), 0), 1.0, 0.0).astype(BF16)
    before = carry_ref[...] + _mm(tri, cnt.astype(BF16))
    rank = jnp.zeros((tm, LANES), jnp.int32)
    for k in range(TOP_K):
        rk = jnp.sum(jnp.where(hot[k], before, 0.0), axis=-1, keepdims=True)
        rank = jnp.where(lane == k, rk.astype(jnp.int32), rank)
    rank_ref[...] = rank
    carry = carry_ref[...] + jnp.sum(cnt, axis=0, keepdims=True)
    carry_ref[...] = carry
    cnt_ref[...] = carry


def _route(ri):
    n = ri.shape[0]
    tm = ROUTE_TM
    return pl.pallas_call(
        _route_kernel,
        grid=(n // tm,),
        in_specs=[pl.BlockSpec((tm, LANES), lambda i: (i, 0))],
        out_specs=[pl.BlockSpec((tm, LANES), lambda i: (i, 0)),
                   pl.BlockSpec((1, LANES), lambda i: (0, 0))],
        out_shape=[jax.ShapeDtypeStruct((n, LANES), jnp.int32),
                   jax.ShapeDtypeStruct((1, LANES), F32)],
        scratch_shapes=[pltpu.VMEM((1, LANES), F32)],
        compiler_params=_cparams(("arbitrary",)),
        name="route",
    )(ri)


def _tile_rows(ref, row):
    return ref.at[pl.ds(pl.multiple_of(row * ROW_TILE, ROW_TILE), ROW_TILE), :]


def _dispatch_kernel(pos_ref, pad_ref, ht_ref, xs_hbm, zero_ref, sem, pad_sem):
    i = pl.program_id(0)
    tm = ht_ref.shape[0] // ROW_TILE

    def issue(r, carry):
        for k in range(TOP_K):
            p = pos_ref[(i * tm + r) * TOP_K + k]
            pltpu.make_async_copy(_tile_rows(ht_ref, r), _tile_rows(xs_hbm, p), sem).start(
                priority=k % 2)
        return carry

    lax.fori_loop(0, tm, issue, 0, unroll=GATHER_UNROLL // TOP_K)

    @pl.when(i == 0)
    def _():
        zero_ref[...] = jnp.zeros_like(zero_ref)
        n_span = (pad_ref.shape[0] - 1) // 2

        def per_span(e, carry):
            start = pad_ref[e]

            def fill(j, c):
                pltpu.make_async_copy(zero_ref, _tile_rows(xs_hbm, start + j), pad_sem).start()
                return c

            return lax.fori_loop(0, pad_ref[n_span + e], fill, carry)

        lax.fori_loop(0, n_span, per_span, 0)

        def drain(j, c):
            pltpu.make_async_copy(zero_ref, _tile_rows(xs_hbm, 0), pad_sem).wait()
            return c

        lax.fori_loop(0, pad_ref[2 * n_span], drain, 0)

    for k in range(TOP_K):
        pltpu.make_async_copy(ht_ref, xs_hbm.at[pl.ds(0, tm * ROW_TILE), :], sem).wait()


def _dispatch(pos, pad_tab, ht, n_rows_pad):
    n = ht.shape[0] // ROW_TILE
    tm = CMB_TM
    grid_spec = pltpu.PrefetchScalarGridSpec(
        num_scalar_prefetch=2,
        grid=(n // tm,),
        in_specs=[pl.BlockSpec((tm * ROW_TILE, LANES), lambda i, pos, pad: (i, 0))],
        out_specs=pl.BlockSpec(memory_space=pl.ANY),
        scratch_shapes=[pltpu.VMEM((ROW_TILE, LANES), F32), pltpu.SemaphoreType.DMA,
                        pltpu.SemaphoreType.DMA],
    )
    return pl.pallas_call(
        _dispatch_kernel,
        grid_spec=grid_spec,
        out_shape=jax.ShapeDtypeStruct((n_rows_pad * ROW_TILE, LANES), F32),
        compiler_params=_cparams(("arbitrary",)),
        name="dispatch",
    )(pos, pad_tab, ht)


def _experts_kernel(be_ref, nu_ref, xs_ref, wgu_ref, bgu_ref, wdn_ref, bdn_ref, y_ref):
    i = pl.program_id(0)
    bm = xs_ref.shape[0] // ROW_TILE

    @pl.when(i < nu_ref[0])
    def _():
        xb = _load_row_tiles(xs_ref, bm).astype(BF16)
        h = _mm(xb, wgu_ref[...]) + bgu_ref[...]
        gate = jnp.minimum(h[:, :D_FF], SWIGLU_LIMIT)
        up = jnp.clip(h[:, D_FF:], -SWIGLU_LIMIT, SWIGLU_LIMIT)
        act = gate * jax.nn.sigmoid(SWIGLU_ALPHA * gate) * (up + 1.0)
        _store_row_tiles(y_ref, _mm(act.astype(BF16), wdn_ref[...]) + bdn_ref[...])

    @pl.when(i >= nu_ref[0])
    def _():
        y_ref[...] = jnp.zeros_like(y_ref)


def _experts(blk_expert, n_used, xs, wgu, bgu, wdn, bdn):
    nblk = blk_expert.shape[0]
    bm = MOE_BM
    used = lambda i, nu: jnp.minimum(i, nu[0] - 1)
    grid_spec = pltpu.PrefetchScalarGridSpec(
        num_scalar_prefetch=2,
        grid=(nblk,),
        in_specs=[pl.BlockSpec((bm * ROW_TILE, LANES), lambda i, be, nu: (used(i, nu), 0)),
                  pl.BlockSpec((None, D_MODEL, 2 * D_FF), lambda i, be, nu: (be[used(i, nu)], 0, 0)),
                  pl.BlockSpec((None, 1, 2 * D_FF), lambda i, be, nu: (be[used(i, nu)], 0, 0)),
                  pl.BlockSpec((None, D_FF, D_MODEL), lambda i, be, nu: (be[used(i, nu)], 0, 0)),
                  pl.BlockSpec((None, 1, D_MODEL), lambda i, be, nu: (be[used(i, nu)], 0, 0))],
        out_specs=pl.BlockSpec((bm * ROW_TILE, LANES), lambda i, be, nu: (i, 0)),
    )
    return pl.pallas_call(
        _experts_kernel,
        grid_spec=grid_spec,
        out_shape=jax.ShapeDtypeStruct((nblk * bm * ROW_TILE, LANES), F32),
        compiler_params=_cparams(("arbitrary",)),
        name="experts",
    )(blk_expert, n_used, xs, wgu, bgu, wdn, bdn)


def _combine_kernel(pos_ref, y_hbm, h_ref, rw_ref, g_ref, b_ref, out_ref, ybuf, sem):
    i = pl.program_id(0)
    nstep = pl.num_programs(0)
    tm = ybuf.shape[2] // ROW_TILE

    def gather(step, slot):
        def issue(r, carry):
            for k in range(TOP_K):
                p = pos_ref[(step * tm + r) * TOP_K + k]
                pltpu.make_async_copy(_tile_rows(y_hbm, p), _tile_rows(ybuf.at[slot, k], r),
                                      sem.at[slot]).start(priority=k % 2)
            return carry
        lax.fori_loop(0, tm, issue, 0, unroll=GATHER_UNROLL // TOP_K)

    @pl.when(i == 0)
    def _():
        gather(0, 0)

    @pl.when(i + 1 < nstep)
    def _():
        gather(i + 1, (i + 1) % 2)

    slot = i % 2
    for k in range(TOP_K):
        pltpu.make_async_copy(y_hbm.at[pl.ds(0, tm * ROW_TILE), :], ybuf.at[slot, k],
                              sem.at[slot]).wait()

    rw = rw_ref[...]
    ffn = rw[:, 0:1] * _load_row_tiles(ybuf.at[slot, 0], tm)
    for k in range(1, TOP_K):
        ffn = ffn + rw[:, k:k + 1] * _load_row_tiles(ybuf.at[slot, k], tm)
    out_ref[...] = _layer_norm(DN_ALPHA * h_ref[...] + ffn, g_ref[...], b_ref[...])


def _combine(pos, ys, h1, rw, g, b):
    n = h1.shape[0]
    tm = CMB_TM
    grid_spec = pltpu.PrefetchScalarGridSpec(
        num_scalar_prefetch=1,
        grid=(n // tm,),
        in_specs=[pl.BlockSpec(memory_space=pl.ANY),
                  pl.BlockSpec((tm, D_MODEL), lambda i, pos: (i, 0)),
                  pl.BlockSpec((tm, LANES), lambda i, pos: (i, 0)),
                  pl.BlockSpec((1, D_MODEL), lambda i, pos: (0, 0)),
                  pl.BlockSpec((1, D_MODEL), lambda i, pos: (0, 0))],
        out_specs=pl.BlockSpec((tm, D_MODEL), lambda i, pos: (i, 0)),
        scratch_shapes=[pltpu.VMEM((2, TOP_K, tm * ROW_TILE, LANES), F32),
                        pltpu.SemaphoreType.DMA((2,))],
    )
    return pl.pallas_call(
        _combine_kernel,
        grid_spec=grid_spec,
        out_shape=jax.ShapeDtypeStruct((n, D_MODEL), F32),
        compiler_params=_cparams(("arbitrary",)),
        name="combine",
    )(pos, ys, h1, rw, g, b)


def _rope_tables(pos):
    inv = ROPE_THETA ** (-jnp.arange(0, ROPE_DIM, 2, dtype=F32) / ROPE_DIM)
    ang = pos.astype(F32)[..., None] * inv
    cos, sin = jnp.cos(ang), jnp.sin(ang)
    pad = jnp.zeros(ang.shape[:-1] + (ATT_HEAD_DIM - ROPE_DIM,), F32)
    c64 = jnp.concatenate([cos, cos, pad + 1.0], axis=-1)
    s1_64 = jnp.concatenate([-sin, jnp.zeros_like(sin), pad], axis=-1)
    s2_64 = jnp.concatenate([jnp.zeros_like(sin), sin, pad], axis=-1)
    dup = lambda a: jnp.concatenate([a, a], axis=-1)
    return dup(c64), dup(s1_64), dup(s2_64)


def _expand_cmp_weights(w1, w2, pe):
    g = ATT_KV_GROUPS
    eye = jnp.eye(g, dtype=F32)
    w1r = w1.reshape(L_CMP, ATT_HEAD_DIM, CMP_HIDDEN)
    w1x = jnp.einsum('ldh,ab->ladbh', w1r, eye).reshape(L_CMP, g * ATT_HEAD_DIM, g * CMP_HIDDEN)
    half = STRIDE_CMP
    top = w1x[:half].reshape(half * g * ATT_HEAD_DIM, g * CMP_HIDDEN).astype(BF16)
    bot = w1x[half:].reshape(half * g * ATT_HEAD_DIM, g * CMP_HIDDEN).astype(BF16)
    w2x = jnp.einsum('hd,ab->ahbd', w2, eye).reshape(g * CMP_HIDDEN, g * ATT_HEAD_DIM).astype(BF16)
    pex = jnp.broadcast_to(pe[:, None, :], (L_CMP, g, ATT_HEAD_DIM))
    pet = pex[:half].reshape(1, half * g * ATT_HEAD_DIM)
    peb = pex[half:].reshape(1, half * g * ATT_HEAD_DIM)
    return top, bot, w2x, pet, peb


def _overlap_matrix(nc_rows, seq):
    n_cmp = (seq - L_CMP) // STRIDE_CMP + 1
    cs = np.arange(nc_rows) * STRIDE_CMP
    ss = np.arange(LANES) * L_SEL
    ov = ((cs[:, None] < ss[None, :] + L_SEL) & (ss[None, :] < cs[:, None] + L_CMP)
          & (np.arange(nc_rows)[:, None] < n_cmp) & (ss[None, :] < seq))
    return jnp.asarray(ov.astype(np.float32), dtype=BF16)


def _block_mask_keys(seq):
    hot = (np.arange(seq)[:, None] // L_SEL) == np.arange(LANES)[None, :]
    return jnp.asarray(np.where(hot, NEG, 0.0).astype(np.float32), dtype=BF16)


def _route_tables(top_i, rank, counts, nblk, bm):
    experts = jnp.arange(N_EXPERTS, dtype=jnp.int32)
    padded = (counts + bm - 1) // bm * bm
    pends = jnp.cumsum(padded)
    pstarts = pends - padded
    onehot = top_i[:, :, None] == experts[None, None, :]
    pos = rank + jnp.sum(jnp.where(onehot, pstarts[None, None, :], 0), axis=-1)
    blk_start = jnp.arange(nblk, dtype=jnp.int32) * bm
    blk_expert = jnp.minimum(jnp.sum((pends[None, :] <= blk_start[:, None]).astype(jnp.int32), axis=1),
                             N_EXPERTS - 1)
    n_used = (pends[-1] // bm).reshape(1)
    n_rows_pad = nblk * bm
    starts = jnp.concatenate([pstarts + counts, pends[-1:]])
    lens = jnp.concatenate([padded - counts, n_rows_pad - pends[-1:]])
    pad_tab = jnp.concatenate([starts, lens, jnp.sum(lens).reshape(1)])
    return pos.reshape(-1).astype(jnp.int32), blk_expert, n_used.astype(jnp.int32), pad_tab.astype(jnp.int32)


def kernel(x, positions, w_in, pe_cmp, w_ck1, w_ck2, w_cv1, w_cv2, hg_lb, hg_norm_g, w_o, ln1_g,
           ln1_b, w_router, b_router, w_gate_up, b_gate_up, w_down, b_down, ln2_g, ln2_b):
    batch, seq, d = x.shape
    n = batch * seq
    assert d == D_MODEL and seq % SEL_TK == 0 and seq % HG_TC == 0 and n % IN_TM == 0
    assert seq // L_SEL <= LANES and seq >= WINDOW + Q_BLOCK
    l = 0
    x2 = x.reshape(n, d)

    n_att = ATT_WIDTH + 6 * KV_WIDTH
    w_pad = jnp.concatenate(
        [w_in[l][:, :n_att + N_GATE], jnp.zeros((d, LANES - N_GATE), F32), w_in[l][:, n_att + N_GATE:]],
        axis=1).astype(BF16)
    wkt, wkb, wk2, pet, peb = _expand_cmp_weights(w_ck1[l], w_ck2[l], pe_cmp[l])
    wvt, wvb, wv2, _, _ = _expand_cmp_weights(w_cv1[l], w_cv2[l], pe_cmp[l])
    lbs = jnp.cumsum(jax.nn.softmax(hg_lb.astype(F32), axis=0), axis=0)[l].reshape(1, HG_HEADS * HG_DK)
    wr_pad = jnp.concatenate([w_router[l], jnp.zeros((d, LANES - N_EXPERTS), F32)], axis=1)
    wr3 = jnp.stack(_split3(wr_pad))
    br_pad = jnp.concatenate([b_router[l], jnp.full((LANES - N_EXPERTS,), NEG, F32)]).reshape(1, LANES)

    c_tab, s1_tab, s2_tab = _rope_tables(positions.reshape(n))
    nc_rows = seq // STRIDE_CMP
    cmp_end = jnp.minimum(jnp.arange(nc_rows) * STRIDE_CMP + L_CMP - 1, seq - 1)
    cc, s1c, s2c = _rope_tables(positions[:, cmp_end])

    qx, kcmp, vcmp, kva, gate, hg = _in_proj(x2, w_pad, c_tab, s1_tab, s2_tab)
    kc, vc = _compress(kcmp.reshape(batch, nc_rows, STRIDE_CMP * LANES),
                       vcmp.reshape(batch, nc_rows, STRIDE_CMP * LANES),
                       pet, peb, wkt, wkb, wvt, wvb, wk2, wv2, cc, s1c, s2c)
    att = _nsa(qx, gate, kc, vc, kva, _overlap_matrix(nc_rows, seq), _block_mask_keys(seq),
               batch, seq)
    rec = _hgrn2(hg, lbs, hg_norm_g[l].reshape(1, HG_DV), batch, seq)
    h1, h1t, ri, rw = _out_proj(att, rec, x2, w_o[l].astype(BF16), ln1_g[l].reshape(1, d),
                                ln1_b[l].reshape(1, d), wr3, br_pad)

    bm = MOE_BM
    a = n * TOP_K
    nblk = -(-(a + N_EXPERTS * (bm - 1)) // bm)
    rank, counts = _route(ri)
    pos, blk_expert, n_used, pad_tab = _route_tables(
        ri[:, :TOP_K], rank[:, :TOP_K], counts[0, :N_EXPERTS].astype(jnp.int32), nblk, bm)
    xs = _dispatch(pos, pad_tab, h1t, nblk * bm)
    ys = _experts(blk_expert, n_used, xs, w_gate_up[l].astype(BF16),
                  b_gate_up[l].reshape(N_EXPERTS, 1, 2 * D_FF), w_down[l].astype(BF16),
                  b_down[l].reshape(N_EXPERTS, 1, d))
    out = _combine(pos, ys, h1, rw, ln2_g[l].reshape(1, d), ln2_b[l].reshape(1, d))
    return out.reshape(batch, seq, d)
```

```python
import functools

import numpy as np
import jax
import jax.numpy as jnp
from jax import lax
from jax.experimental import pallas as pl
from jax.experimental.pallas import tpu as pltpu

F32 = jnp.float32
BF16 = jnp.bfloat16

D_MODEL = 1024
ATT_HEADS = 8
ATT_HEAD_DIM = 64
ATT_KV_GROUPS = 2
ATT_HPG = ATT_HEADS // ATT_KV_GROUPS
ATT_WIDTH = ATT_HEADS * ATT_HEAD_DIM
KV_WIDTH = ATT_KV_GROUPS * ATT_HEAD_DIM
N_BRANCH = 3
L_CMP = 32
STRIDE_CMP = 16
CMP_HIDDEN = 256
L_SEL = 64
N_SELECT = 16
N_FORCED_LOCAL = 2
FORCE_BONUS = 1000.0
WINDOW = 512
Q_BLOCK = 128
ROPE_THETA = 500000.0
ROPE_DIM = ATT_HEAD_DIM // 4
ROPE_HALF = ROPE_DIM // 2
HG_HEADS = 4
HG_DK = 128
HG_DV = 128
HG_WIDTH = HG_HEADS * HG_DV
HG_CHUNK = 64
HG_SUB = 16
HG_SAFE_DECAY = 60.0
N_EXPERTS = 32
TOP_K = 4
D_FF = 1024
SWIGLU_LIMIT = 7.0
SWIGLU_ALPHA = 1.702
DEPTH = 1
DN_ALPHA = (2 * DEPTH) ** 0.25
LN_EPS = 1e-5
RMS_EPS = 1e-6
NEG = -1e30
LOG2E = 1.4426950408889634
SEL_SHIFT = L_SEL.bit_length() - 1
SUB_SHIFT = HG_SUB.bit_length() - 1

LANES = 128
VMEM_LIMIT = 56 * 1024 * 1024
IN_TM = 512
SEL_TK = 512
SEL_TK_BULK = 1024
HG_TC = 512
MOE_BM = 256
CMB_TM = 256
GATHER_UNROLL = 8
ROW_TILE = D_MODEL // LANES
ROUTE_TM = 512

C_Q = 0
C_KCMP = 512
C_VCMP = 640
C_KVA = 768
C_GATE = 1280
C_HG = 1408
IN_COLS = C_HG + 4 * HG_WIDTH
N_GATE = ATT_HEADS * N_BRANCH


def _nt(a, b):
    return lax.dot_general(a, b, (((1,), (1,)), ((), ())), preferred_element_type=F32)


def _tn(a, b):
    return lax.dot_general(a, b, (((0,), (0,)), ((), ())), preferred_element_type=F32)


def _mm(a, b):
    return jnp.dot(a, b, preferred_element_type=F32)


def _split3(x):
    hi = x.astype(BF16)
    r1 = x - hi.astype(F32)
    mid = r1.astype(BF16)
    lo = (r1 - mid.astype(F32)).astype(BF16)
    return hi, mid, lo


def _rope(v, c, s1, s2):
    return v * c + pltpu.roll(v, LANES - ROPE_HALF, 1) * s1 + pltpu.roll(v, ROPE_HALF, 1) * s2


def _cparams(sem):
    return pltpu.CompilerParams(dimension_semantics=sem, vmem_limit_bytes=VMEM_LIMIT)


def _rope_lane_tables(cs):
    lane = lax.broadcasted_iota(jnp.int32, (1, LANES), 1)
    at = lambda d0: (lane >= d0) & (lane < d0 + ROPE_HALF)
    h1 = ATT_HEAD_DIM
    from_lane0 = lambda shift: pltpu.roll(cs, shift % LANES, 1)
    r_h1 = from_lane0(h1)
    c = jnp.where(at(0), cs, jnp.where(at(ROPE_HALF), from_lane0(ROPE_HALF),
        jnp.where(at(h1), r_h1, jnp.where(at(h1 + ROPE_HALF), from_lane0(h1 + ROPE_HALF), 1.0))))
    s1 = jnp.where(at(0), -from_lane0(-ROPE_HALF), jnp.where(at(h1), -from_lane0(h1 - ROPE_HALF), 0.0))
    s2 = jnp.where(at(ROPE_HALF), cs, jnp.where(at(h1 + ROPE_HALF), r_h1, 0.0))
    return c, s1, s2


def _in_proj_kernel(x_ref, w_ref, cs_ref,
                    qx_ref, kcmp_ref, vcmp_ref, kva_ref, gate_ref, hg_ref):
    xb = x_ref[...].astype(BF16)
    c, s1, s2 = _rope_lane_tables(cs_ref[...])
    lane = lax.broadcasted_iota(jnp.int32, (1, LANES), 1)
    low = lane < ATT_HEAD_DIM

    q = _mm(xb, w_ref[:, C_Q:C_Q + ATT_WIDTH])
    scale = ATT_HEAD_DIM ** -0.5 * LOG2E
    for j in range(ATT_WIDTH // LANES):
        v = _rope(q[:, j * LANES:(j + 1) * LANES], c, s1, s2) * scale
        vr = pltpu.roll(v, ATT_HEAD_DIM, 1)
        g = (2 * j) // ATT_HPG
        keep = low if g == 0 else jnp.logical_not(low)
        h0 = jnp.where(keep, v if g == 0 else vr, 0.0)
        h1 = jnp.where(keep, vr if g == 0 else v, 0.0)
        qx_ref[:, (2 * j) * LANES:(2 * j + 1) * LANES] = h0.astype(BF16)
        qx_ref[:, (2 * j + 1) * LANES:(2 * j + 2) * LANES] = h1.astype(BF16)

    kv = _mm(xb, w_ref[:, C_KCMP:C_GATE])
    kcmp_ref[...] = kv[:, 0:128]
    vcmp_ref[...] = kv[:, 128:256]
    kva_ref[:, 0:128] = _rope(kv[:, 256:384], c, s1, s2).astype(BF16)
    kva_ref[:, 128:256] = kv[:, 384:512].astype(BF16)
    kva_ref[:, 256:384] = _rope(kv[:, 512:640], c, s1, s2).astype(BF16)
    kva_ref[:, 384:512] = kv[:, 640:768].astype(BF16)

    gate_ref[...] = jax.nn.sigmoid(_mm(xb, w_ref[:, C_GATE:C_HG]))
    for j in range(4):
        hg_ref[:, j * HG_WIDTH:(j + 1) * HG_WIDTH] = _mm(
            xb, w_ref[:, C_HG + j * HG_WIDTH:C_HG + (j + 1) * HG_WIDTH])


def _in_proj(x2, w_pad, cs_tab):
    n = x2.shape[0]
    tm = IN_TM
    row = lambda w: pl.BlockSpec((tm, w), lambda i: (i, 0))
    return pl.pallas_call(
        _in_proj_kernel,
        grid=(n // tm,),
        in_specs=[row(D_MODEL), pl.BlockSpec((D_MODEL, IN_COLS), lambda i: (0, 0)),
                  row(LANES)],
        out_specs=[row(ATT_HEADS * LANES), row(LANES), row(LANES), row(4 * LANES), row(LANES),
                   row(4 * HG_WIDTH)],
        out_shape=[jax.ShapeDtypeStruct((n, ATT_HEADS * LANES), BF16),
                   jax.ShapeDtypeStruct((n, LANES), F32),
                   jax.ShapeDtypeStruct((n, LANES), F32),
                   jax.ShapeDtypeStruct((n, 4 * LANES), BF16),
                   jax.ShapeDtypeStruct((n, LANES), F32),
                   jax.ShapeDtypeStruct((n, 4 * HG_WIDTH), F32)],
        compiler_params=_cparams(("arbitrary",)),
        name="in_proj",
    )(x2, w_pad, cs_tab)


def _compress_kernel(kf_ref, vf_ref, pet_ref, peb_ref, wkt_ref, wkb_ref, wvt_ref, wvb_ref,
                     wk2_ref, wv2_ref, cs_ref, kc_ref, vc_ref):
    nc = kf_ref.shape[0]

    def mlp(x, wt, wb, w2):
        top = (x + pet_ref[...]).astype(BF16)
        bot = (x + peb_ref[...]).astype(BF16)
        u = _mm(top, wt[...])
        v = _mm(bot, wb[...])
        h = u + pltpu.roll(v, nc - 1, 0)
        return _mm(jax.nn.gelu(h).astype(BF16), w2[...])

    kc = mlp(kf_ref[...], wkt_ref, wkb_ref, wk2_ref)
    kc_ref[...] = _rope(kc, *_rope_lane_tables(cs_ref[...])).astype(BF16)
    vc_ref[...] = mlp(vf_ref[...], wvt_ref, wvb_ref, wv2_ref).astype(BF16)


def _compress(kf, vf, pet, peb, wkt, wkb, wvt, wvb, wk2, wv2, cs_cmp):
    b, nc, w = kf.shape
    per_b = lambda width: pl.BlockSpec((None, nc, width), lambda i: (i, 0, 0))
    full = lambda a: pl.BlockSpec(a.shape, lambda i: (0,) * a.ndim)
    return pl.pallas_call(
        _compress_kernel,
        grid=(b,),
        in_specs=[per_b(w), per_b(w), full(pet), full(peb), full(wkt), full(wkb), full(wvt),
                  full(wvb), full(wk2), full(wv2), per_b(LANES)],
        out_specs=[per_b(LANES), per_b(LANES)],
        out_shape=[jax.ShapeDtypeStruct((b, nc, LANES), BF16)] * 2,
        compiler_params=_cparams(("arbitrary",)),
        name="compress",
    )(kf, vf, pet, peb, wkt, wkb, wvt, wvb, wk2, wv2, cs_cmp)


def _nsa_kernel(qx_ref, gate_ref, kc_ref, vc_ref, kva_ref, ovl_ref, ene_ref, out_ref,
                m_ref, acc_ref, *, seq):
    t = Q_BLOCK
    rows = ATT_HPG * t
    qb = pl.program_id(1)
    s0 = qb * t
    t_pos = s0 + lax.broadcasted_iota(jnp.int32, (t, 1), 0)
    nc = kc_ref.shape[0]
    gates = gate_ref[...]
    lane = lax.broadcasted_iota(jnp.int32, (1, LANES), 1)

    def tile4(a):
        return jnp.concatenate([a] * ATT_HPG, axis=0)

    def per_head(fn, s):
        return jnp.concatenate([fn(s[n * t:(n + 1) * t]) for n in range(ATT_HEADS)], axis=0)

    def softmax_num(s):
        return jnp.exp2(s - jnp.max(s, axis=-1, keepdims=True)).astype(BF16)

    cmp_end = lax.broadcasted_iota(jnp.int32, (1, nc), 1) * STRIDE_CMP + (L_CMP - 1)
    bias_c = jnp.where((cmp_end <= t_pos) & (cmp_end < seq), 0.0, NEG)
    has_cmp = tile4(t_pos >= L_CMP - 1)
    wlen = WINDOW + t
    ws = pl.multiple_of(jnp.maximum(s0 - WINDOW, 0), t)
    kp = ws + lax.broadcasted_iota(jnp.int32, (1, wlen), 1)
    bias_w = jnp.where((kp <= t_pos) & (kp > t_pos - WINDOW), 0.0, NEG)
    dist = jnp.right_shift(t_pos, SEL_SHIFT) - lane
    causal_blk = dist >= 0
    bonus = jnp.where((lane == 0) | (causal_blk & (dist < N_FORCED_LOCAL)), FORCE_BONUS, 0.0)
    brow = lax.broadcasted_iota(jnp.int32, (LANES, ATT_KV_GROUPS * t), 0)

    def with_ones(v, g):
        in_g = (lane >= g * ATT_HEAD_DIM) & (lane < (g + 1) * ATT_HEAD_DIM)
        return v * jnp.where(in_g, 1.0, 0.0).astype(BF16) + jnp.where(in_g, 0.0, 1.0).astype(BF16)

    groups = range(ATT_KV_GROUPS)
    den = [(1 - g) * ATT_HEAD_DIM for g in groups]
    q_all = jnp.concatenate([qx_ref[:, h * LANES:(h + 1) * LANES] for h in range(ATT_HEADS)],
                            axis=0)

    p_c = softmax_num(per_head(lambda a: a + bias_c, _nt(q_all, kc_ref[...])))
    o_cs, imps = [], []
    for g in groups:
        oi = _mm(p_c[g * rows:(g + 1) * rows],
                 jnp.concatenate([with_ones(vc_ref[...], g), ovl_ref[...]], axis=1))
        inv_c = jnp.where(has_cmp, 1.0 / jnp.maximum(oi[:, den[g]:den[g] + 1], 1e-20), 0.0)
        o_cs.append(oi[:, :LANES] * inv_c)
        imp4 = oi[:, LANES:] * inv_c
        imp = imp4[0:t]
        for n in range(1, ATT_HPG):
            imp = imp + imp4[n * t:(n + 1) * t]
        imps.append(jnp.where(causal_blk, imp + bonus, -1.0))

    imp_t = jnp.concatenate(imps, axis=0).T
    sel_t = jnp.zeros(imp_t.shape, F32)
    for _ in range(N_SELECT):
        mx = jnp.max(imp_t, axis=0, keepdims=True)
        idx = jnp.min(jnp.where(imp_t == mx, brow, LANES), axis=0, keepdims=True)
        pick = brow == idx
        sel_t = jnp.where(pick, 1.0, sel_t)
        imp_t = jnp.where(pick, -3e38, imp_t)
    sel = sel_t.T
    unsel = [jnp.where(causal_blk, 1.0 - sel[g * t:(g + 1) * t], 1.0).astype(BF16) for g in groups]

    lhs = jnp.concatenate([q_all, jnp.concatenate([tile4(u) for u in unsel], axis=0)],
                          axis=1)
    m_ref[...] = jnp.full(m_ref.shape, NEG, F32)
    acc_ref[...] = jnp.zeros(acc_ref.shape, F32)

    def sel_tile(k0, tk, causal):
        rhs = jnp.concatenate([kva_ref[pl.ds(k0, tk), 0:128], ene_ref[pl.ds(k0, tk), :]],
                              axis=1)
        v_t = kva_ref[pl.ds(k0, tk), 128:256]
        s = _nt(lhs, rhs)
        if causal:
            future = k0 + lax.broadcasted_iota(jnp.int32, (1, tk), 1) > t_pos
            s = per_head(lambda a: jnp.where(future, NEG, a), s)
        m_old = m_ref[...]
        m_new = jnp.maximum(m_old, jnp.max(s, axis=-1, keepdims=True))
        m_ref[...] = m_new
        p = jnp.concatenate([s[:, j * LANES:(j + 1) * LANES] - m_new
                             for j in range(tk // LANES)], axis=1)
        p = jnp.exp2(p).astype(BF16)
        pv = jnp.concatenate([_mm(p[g * rows:(g + 1) * rows], with_ones(v_t, g)) for g in groups],
                             axis=0)
        acc_ref[...] = jnp.exp2(m_old - m_new) * acc_ref[...] + pv

    n_bulk = s0 // SEL_TK_BULK
    n_tail = (s0 + t - n_bulk * SEL_TK_BULK + SEL_TK - 1) // SEL_TK

    def bulk_tile(kt, carry):
        sel_tile(pl.multiple_of(kt * SEL_TK_BULK, SEL_TK_BULK), SEL_TK_BULK, False)
        return carry

    def tail_tile(kt, carry):
        sel_tile(pl.multiple_of(n_bulk * SEL_TK_BULK + kt * SEL_TK, SEL_TK), SEL_TK, True)
        return carry

    lax.fori_loop(0, n_bulk, bulk_tile, 0)
    lax.fori_loop(0, n_tail, tail_tile, 0)

    p_w = softmax_num(per_head(lambda a: a + bias_w,
                               _nt(q_all, kva_ref[pl.ds(ws, wlen), 256:384])))

    pieces = []
    for g in groups:
        acc = acc_ref[g * rows:(g + 1) * rows, :]
        o_s = acc * (1.0 / jnp.maximum(acc[:, den[g]:den[g] + 1], 1e-20))
        o_w = _mm(p_w[g * rows:(g + 1) * rows], with_ones(kva_ref[pl.ds(ws, wlen), 384:512], g))
        o_w = o_w * (1.0 / jnp.maximum(o_w[:, den[g]:den[g] + 1], 1e-20))

        o_c = o_cs[g]
        for n in range(ATT_HPG):
            c0 = (g * ATT_HPG + n) * N_BRANCH
            r = slice(n * t, (n + 1) * t)
            ln = slice(g * ATT_HEAD_DIM, (g + 1) * ATT_HEAD_DIM)
            pieces.append(gates[:, c0:c0 + 1] * o_c[r, ln]
                          + gates[:, c0 + 1:c0 + 2] * o_s[r, ln]
                          + gates[:, c0 + 2:c0 + 3] * o_w[r, ln])
    out_ref[...] = jnp.concatenate(pieces, axis=1).astype(BF16)


def _nsa(qx, gate, kc, vc, kva, ovl, ene, batch, seq):
    n = qx.shape[0]
    nqb = seq // Q_BLOCK
    nc = kc.shape[1]
    row = lambda w: pl.BlockSpec((Q_BLOCK, w), lambda b, i: (b * nqb + i, 0))
    return pl.pallas_call(
        functools.partial(_nsa_kernel, seq=seq),
        grid=(batch, nqb),
        in_specs=[row(ATT_HEADS * LANES), row(LANES),
                  pl.BlockSpec((None, nc, LANES), lambda b, i: (b, 0, 0)),
                  pl.BlockSpec((None, nc, LANES), lambda b, i: (b, 0, 0)),
                  pl.BlockSpec((None, seq, 4 * LANES), lambda b, i: (b, 0, 0)),
                  pl.BlockSpec(ovl.shape, lambda b, i: (0, 0)),
                  pl.BlockSpec(ene.shape, lambda b, i: (0, 0))],
        out_specs=row(ATT_WIDTH),
        out_shape=jax.ShapeDtypeStruct((n, ATT_WIDTH), BF16),
        scratch_shapes=[pltpu.VMEM((ATT_HEADS * Q_BLOCK, LANES), F32),
                        pltpu.VMEM((ATT_HEADS * Q_BLOCK, LANES), F32)],
        compiler_params=_cparams(("arbitrary", "arbitrary")),
        name="nsa",
    )(qx, gate, kc, vc, kva.reshape(batch, seq, 4 * LANES), ovl, ene)


def _bmm(a, b):
    return lax.dot_general(a, b, (((2,), (1,)), ((0,), (0,))), preferred_element_type=F32)


def _bnt(a, b):
    return lax.dot_general(a, b, (((2,), (2,)), ((0,), (0,))), preferred_element_type=F32)


def _hgrn2_kernel(q_ref, f_ref, i_ref, g_ref, lb_ref, ng_ref, out_ref, st_ref, b_ref, kk_ref):
    c = HG_CHUNK
    sub = HG_SUB
    nsub = c // sub
    tc = q_ref.shape[0]
    nch = tc // c

    @pl.when(pl.program_id(2) == 0)
    def _():
        st_ref[...] = jnp.zeros_like(st_ref)

    lb = lb_ref[...]
    f_hat = jax.nn.sigmoid(f_ref[...])
    logf = jnp.log(lb + (1.0 - lb) * f_hat)
    kk_ref[...] = (1.0 - lb) * (1.0 - f_hat)

    ri = lax.broadcasted_iota(jnp.int32, (c, c), 0)
    ci = lax.broadcasted_iota(jnp.int32, (c, c), 1)
    lower = ci <= ri
    ng = ng_ref[...]

    tril3 = jnp.broadcast_to(jnp.where(lower, 1.0, 0.0).astype(BF16)[None], (nch, c, c))
    b3 = None
    for part in _split3(logf):
        term = _bmm(tril3, part.reshape(nch, c, HG_DK))
        b3 = term if b3 is None else b3 + term
    b_ref[...] = b3.reshape(tc, HG_DK)
    worst_decay = jnp.max(-b3[:, c - 1:c, :])

    def finish(o, rows):
        o = o * lax.rsqrt(jnp.mean(o * o, axis=-1, keepdims=True) + RMS_EPS) * ng
        out_ref[rows, :] = (o * jax.nn.silu(g_ref[rows, :])).astype(BF16)

    def whole_chunk_decay():
        b = b_ref[...].reshape(nch, c, HG_DK)
        eb = jnp.exp(b)
        qe = (q_ref[...].reshape(nch, c, HG_DK) * eb).astype(BF16)
        ke = kk_ref[...].reshape(nch, c, HG_DK) * jnp.exp(-b)
        v = i_ref[...].astype(BF16).reshape(nch, c, HG_DV)
        a = jnp.where(lower[None], _bnt(qe, ke.astype(BF16)), 0.0).astype(BF16)
        o_intra = _bmm(a, v)
        eb_last = eb[:, c - 1:c, :]
        kd = (ke * eb_last).astype(BF16)
        st = st_ref[...]
        outs = []
        for ic in range(nch):
            outs.append(o_intra[ic] + _nt(qe[ic], st.astype(BF16)))
            st = st * eb_last[ic] + _tn(v[ic], kd[ic])
        st_ref[...] = st
        finish(jnp.concatenate(outs, axis=0), slice(None))

    def sub_block_decay():
        lax.fori_loop(0, nch, chunk, 0)

    row = lax.broadcasted_iota(jnp.int32, (c, 1), 0)
    row_sub = jnp.right_shift(row, SUB_SHIFT)
    row_loc = row - row_sub * sub

    def chunk(ic, carry):
        r0 = pl.multiple_of(ic * c, c)
        q = q_ref[pl.ds(r0, c), :]
        k = kk_ref[pl.ds(r0, c), :]
        v = i_ref[pl.ds(r0, c), :].astype(BF16)
        b = b_ref[pl.ds(r0, c), :]
        st = st_ref[...]
        o = _nt((q * jnp.exp(b)).astype(BF16), st.astype(BF16))

        a_rows = [jnp.zeros((sub, c), F32)]
        for i in range(1, nsub):
            ref_b = b[i * sub:i * sub + 1, :]
            qs = q[i * sub:(i + 1) * sub, :] * jnp.exp(b[i * sub:(i + 1) * sub, :] - ref_b)
            ks = jnp.where(row < i * sub, k * jnp.exp(jnp.minimum(ref_b - b, 0.0)), 0.0)
            a_rows.append(_nt(qs.astype(BF16), ks.astype(BF16)))
        a = jnp.concatenate(a_rows, axis=0)

        b3 = b.reshape(nsub, sub, HG_DK)
        q3 = q.reshape(nsub, sub, HG_DK)
        k3 = k.reshape(nsub, sub, HG_DK)
        for s in range(sub):
            dec = jnp.exp(jnp.minimum(b3 - b3[:, s:s + 1, :], 0.0))
            w = (q3 * k3[:, s:s + 1, :] * dec).reshape(c, HG_DK)
            col = jnp.sum(w, axis=-1, keepdims=True)
            hit = (ci == row_sub * sub + s) & (row_loc >= s)
            a = a + jnp.where(hit, col, 0.0)

        o = o + _mm(a.astype(BF16), v)
        b_last = b[c - 1:c, :]
        kd = (k * jnp.exp(b_last - b)).astype(BF16)
        st_ref[...] = st * jnp.exp(b_last) + _tn(v, kd)
        finish(o, pl.ds(r0, c))
        return carry

    lax.cond(worst_decay < HG_SAFE_DECAY, whole_chunk_decay, sub_block_decay)


def _hgrn2(hg, lb, ng, batch, seq):
    n = hg.shape[0]
    tc = HG_TC
    nt = seq // tc
    col = lambda j: pl.BlockSpec((tc, HG_DK), lambda b, h, i, j=j: (b * nt + i, j * HG_HEADS + h))
    return pl.pallas_call(
        _hgrn2_kernel,
        grid=(batch, HG_HEADS, nt),
        in_specs=[col(0), col(1), col(2), col(3),
                  pl.BlockSpec((1, HG_DK), lambda b, h, i: (0, h)),
                  pl.BlockSpec((1, HG_DV), lambda b, h, i: (0, 0))],
        out_specs=pl.BlockSpec((tc, HG_DV), lambda b, h, i: (b * nt + i, h)),
        out_shape=jax.ShapeDtypeStruct((n, HG_WIDTH), BF16),
        scratch_shapes=[pltpu.VMEM((HG_DV, HG_DK), F32),
                        pltpu.VMEM((tc, HG_DK), F32),
                        pltpu.VMEM((tc, HG_DK), F32)],
        compiler_params=_cparams(("arbitrary", "arbitrary", "arbitrary")),
        name="hgrn2",
    )(hg, hg, hg, hg, lb, ng)


def _store_row_tiles(ref, v):
    rows = v.shape[0]
    for s in range(ROW_TILE):
        ref[pl.ds(s, rows, stride=ROW_TILE), :] = v[:, s * LANES:(s + 1) * LANES]


def _load_row_tiles(ref, rows):
    return jnp.concatenate([ref[pl.ds(s, rows, stride=ROW_TILE), :] for s in range(ROW_TILE)],
                           axis=1)


def _layer_norm(y, g, b):
    mu = jnp.mean(y, axis=-1, keepdims=True)
    d = y - mu
    var = jnp.mean(d * d, axis=-1, keepdims=True)
    return d * lax.rsqrt(var + LN_EPS) * g + b


def _out_proj_kernel(att_ref, rec_ref, x_ref, wo_ref, g_ref, b_ref, wr_ref, br_ref,
                     h_ref, ht_ref, ri_ref, rw_ref):
    mix = _mm(att_ref[...], wo_ref[0:ATT_WIDTH, :]) + _mm(rec_ref[...], wo_ref[ATT_WIDTH:, :])
    h = _layer_norm(DN_ALPHA * x_ref[...] + mix, g_ref[...], b_ref[...])
    h_ref[...] = h
    _store_row_tiles(ht_ref, h)

    hh, hm, _ = _split3(h)
    hw = _mm(hh, wr_ref[...])
    lg = (hw[:, :LANES] + hw[:, LANES:] + _mm(hm, wr_ref[:, :LANES])
          + br_ref[...])
    lane = lax.broadcasted_iota(jnp.int32, lg.shape, 1)
    ri = jnp.zeros(lg.shape, jnp.int32)
    rw = jnp.zeros(lg.shape, F32)
    v0 = None
    for r in range(TOP_K):
        mx = jnp.max(lg, axis=-1, keepdims=True)
        idx = jnp.min(jnp.where(lg == mx, lane, LANES), axis=-1, keepdims=True)
        if r == 0:
            v0 = mx
        ri = jnp.where(lane == r, idx, ri)
        rw = jnp.where(lane == r, jnp.exp(mx - v0), rw)
        lg = jnp.where(lane == idx, -3e38, lg)
    ri_ref[...] = ri
    rw_ref[...] = rw / jnp.sum(rw, axis=-1, keepdims=True)


def _out_proj(att, rec, x2, wo, g, b, wr3, br):
    n = x2.shape[0]
    tm = IN_TM
    row = lambda w: pl.BlockSpec((tm, w), lambda i: (i, 0))
    full = lambda a: pl.BlockSpec(a.shape, lambda i: (0,) * a.ndim)
    return pl.pallas_call(
        _out_proj_kernel,
        grid=(n // tm,),
        in_specs=[row(ATT_WIDTH), row(HG_WIDTH), row(D_MODEL), full(wo), full(g), full(b),
                  full(wr3), full(br)],
        out_specs=[row(D_MODEL), pl.BlockSpec((tm * ROW_TILE, LANES), lambda i: (i, 0)),
                   row(LANES), row(LANES)],
        out_shape=[jax.ShapeDtypeStruct((n, D_MODEL), F32),
                   jax.ShapeDtypeStruct((n * ROW_TILE, LANES), F32),
                   jax.ShapeDtypeStruct((n, LANES), jnp.int32),
                   jax.ShapeDtypeStruct((n, LANES), F32)],
        compiler_params=_cparams(("arbitrary",)),
        name="out_proj",
    )(att, rec, x2, wo, g, b, wr3, br)


def _route_kernel(ri_ref, rank_ref, cnt_ref, carry_ref):
    tm = ri_ref.shape[0]

    @pl.when(pl.program_id(0) == 0)
    def _():
        carry_ref[...] = jnp.zeros_like(carry_ref)

    ri = ri_ref[...]
    lane = lax.broadcasted_iota(jnp.int32, (tm, LANES), 1)
    hot = [lane == ri[:, k:k + 1] for k in range(TOP_K)]
    cnt = jnp.where(hot[0], 1.0, 0.0)
    for k in range(1, TOP_K):
        cnt = cnt + jnp.where(hot[k], 1.0, 0.0)
    tri = jnp.where(lax.broadcasted_iota(jnp.int32, (tm, tm), 1)
                    < lax.broadcasted_iota(jnp.int32, (tm, tm), 0), 1.0, 0.0).astype(BF16)
    before = carry_ref[...] + _mm(tri, cnt.astype(BF16))
    rank = jnp.zeros((tm, LANES), jnp.int32)
    for k in range(TOP_K):
        rk = jnp.sum(jnp.where(hot[k], before, 0.0), axis=-1, keepdims=True)
        rank = jnp.where(lane == k, rk.astype(jnp.int32), rank)
    rank_ref[...] = rank
    carry = carry_ref[...] + jnp.sum(cnt, axis=0, keepdims=True)
    carry_ref[...] = carry
    cnt_ref[...] = carry


def _route(ri):
    n = ri.shape[0]
    tm = ROUTE_TM
    return pl.pallas_call(
        _route_kernel,
        grid=(n // tm,),
        in_specs=[pl.BlockSpec((tm, LANES), lambda i: (i, 0))],
        out_specs=[pl.BlockSpec((tm, LANES), lambda i: (i, 0)),
                   pl.BlockSpec((1, LANES), lambda i: (0, 0))],
        out_shape=[jax.ShapeDtypeStruct((n, LANES), jnp.int32),
                   jax.ShapeDtypeStruct((1, LANES), F32)],
        scratch_shapes=[pltpu.VMEM((1, LANES), F32)],
        compiler_params=_cparams(("arbitrary",)),
        name="route",
    )(ri)


def _tile_rows(ref, row):
    return ref.at[pl.ds(pl.multiple_of(row * ROW_TILE, ROW_TILE), ROW_TILE), :]


def _dispatch_kernel(pos_ref, pad_ref, ht_ref, xs_hbm, zero_ref, sem, pad_sem):
    i = pl.program_id(0)
    tm = ht_ref.shape[0] // ROW_TILE

    def issue(r, carry):
        for k in range(TOP_K):
            p = pos_ref[(i * tm + r) * TOP_K + k]
            pltpu.make_async_copy(_tile_rows(ht_ref, r), _tile_rows(xs_hbm, p), sem).start(
                priority=k % 2)
        return carry

    lax.fori_loop(0, tm, issue, 0, unroll=GATHER_UNROLL // TOP_K)

    @pl.when(i == 0)
    def _():
        zero_ref[...] = jnp.zeros_like(zero_ref)
        n_span = (pad_ref.shape[0] - 1) // 2

        def per_span(e, carry):
            start = pad_ref[e]

            def fill(j, c):
                pltpu.make_async_copy(zero_ref, _tile_rows(xs_hbm, start + j), pad_sem).start()
                return c

            return lax.fori_loop(0, pad_ref[n_span + e], fill, carry)

        lax.fori_loop(0, n_span, per_span, 0)

        def drain(j, c):
            pltpu.make_async_copy(zero_ref, _tile_rows(xs_hbm, 0), pad_sem).wait()
            return c

        lax.fori_loop(0, pad_ref[2 * n_span], drain, 0)

    for k in range(TOP_K):
        pltpu.make_async_copy(ht_ref, xs_hbm.at[pl.ds(0, tm * ROW_TILE), :], sem).wait()


def _dispatch(pos, pad_tab, ht, n_rows_pad):
    n = ht.shape[0] // ROW_TILE
    tm = CMB_TM
    grid_spec = pltpu.PrefetchScalarGridSpec(
        num_scalar_prefetch=2,
        grid=(n // tm,),
        in_specs=[pl.BlockSpec((tm * ROW_TILE, LANES), lambda i, pos, pad: (i, 0))],
        out_specs=pl.BlockSpec(memory_space=pl.ANY),
        scratch_shapes=[pltpu.VMEM((ROW_TILE, LANES), F32), pltpu.SemaphoreType.DMA,
                        pltpu.SemaphoreType.DMA],
    )
    return pl.pallas_call(
        _dispatch_kernel,
        grid_spec=grid_spec,
        out_shape=jax.ShapeDtypeStruct((n_rows_pad * ROW_TILE, LANES), F32),
        compiler_params=_cparams(("arbitrary",)),
        name="dispatch",
    )(pos, pad_tab, ht)


def _experts_kernel(be_ref, nu_ref, xs_ref, wgu_ref, bgu_ref, wdn_ref, bdn_ref, y_ref,
                    wgu_bf, wdn_bf):
    i = pl.program_id(0)
    bm = xs_ref.shape[0] // ROW_TILE

    @pl.when(i < nu_ref[0])
    def _():
        @pl.when((i == 0) | (be_ref[i] != be_ref[jnp.maximum(i - 1, 0)]))
        def _():
            def cast_rows(r, carry):
                rows = pl.ds(pl.multiple_of(r * LANES, LANES), LANES)
                wgu_bf[rows, :] = wgu_ref[rows, :].astype(BF16)
                wdn_bf[rows, :] = wdn_ref[rows, :].astype(BF16)
                return carry
            lax.fori_loop(0, D_MODEL // LANES, cast_rows, 0)

        xb = _load_row_tiles(xs_ref, bm).astype(BF16)
        h = _mm(xb, wgu_bf[...]) + bgu_ref[...]
        gate = jnp.minimum(h[:, :D_FF], SWIGLU_LIMIT)
        up = jnp.clip(h[:, D_FF:], -SWIGLU_LIMIT, SWIGLU_LIMIT)
        act = gate * jax.nn.sigmoid(SWIGLU_ALPHA * gate) * (up + 1.0)
        _store_row_tiles(y_ref, _mm(act.astype(BF16), wdn_bf[...]) + bdn_ref[...])

    @pl.when(i >= nu_ref[0])
    def _():
        y_ref[...] = jnp.zeros_like(y_ref)


def _experts(blk_expert, n_used, xs, wgu, bgu, wdn, bdn):
    nblk = blk_expert.shape[0]
    bm = MOE_BM
    used = lambda i, nu: jnp.minimum(i, nu[0] - 1)
    grid_spec = pltpu.PrefetchScalarGridSpec(
        num_scalar_prefetch=2,
        grid=(nblk,),
        in_specs=[pl.BlockSpec((bm * ROW_TILE, LANES), lambda i, be, nu: (used(i, nu), 0)),
                  pl.BlockSpec((None, D_MODEL, 2 * D_FF), lambda i, be, nu: (be[used(i, nu)], 0, 0)),
                  pl.BlockSpec((None, 1, 2 * D_FF), lambda i, be, nu: (be[used(i, nu)], 0, 0)),
                  pl.BlockSpec((None, D_FF, D_MODEL), lambda i, be, nu: (be[used(i, nu)], 0, 0)),
                  pl.BlockSpec((None, 1, D_MODEL), lambda i, be, nu: (be[used(i, nu)], 0, 0))],
        out_specs=pl.BlockSpec((bm * ROW_TILE, LANES), lambda i, be, nu: (i, 0)),
        scratch_shapes=[pltpu.VMEM((D_MODEL, 2 * D_FF), BF16), pltpu.VMEM((D_FF, D_MODEL), BF16)],
    )
    return pl.pallas_call(
        _experts_kernel,
        grid_spec=grid_spec,
        out_shape=jax.ShapeDtypeStruct((nblk * bm * ROW_TILE, LANES), F32),
        compiler_params=_cparams(("arbitrary",)),
        name="experts",
    )(blk_expert, n_used, xs, wgu, bgu, wdn, bdn)


def _combine_kernel(pos_ref, y_hbm, h_ref, rw_ref, g_ref, b_ref, out_ref, ybuf, sem):
    i = pl.program_id(0)
    nstep = pl.num_programs(0)
    tm = ybuf.shape[2] // ROW_TILE

    def gather(step, slot):
        def issue(r, carry):
            for k in range(TOP_K):
                p = pos_ref[(step * tm + r) * TOP_K + k]
                pltpu.make_async_copy(_tile_rows(y_hbm, p), _tile_rows(ybuf.at[slot, k], r),
                                      sem.at[slot]).start(priority=k % 2)
            return carry
        lax.fori_loop(0, tm, issue, 0, unroll=GATHER_UNROLL // TOP_K)

    @pl.when(i == 0)
    def _():
        gather(0, 0)

    @pl.when(i + 1 < nstep)
    def _():
        gather(i + 1, (i + 1) % 2)

    slot = i % 2
    for k in range(TOP_K):
        pltpu.make_async_copy(y_hbm.at[pl.ds(0, tm * ROW_TILE), :], ybuf.at[slot, k],
                              sem.at[slot]).wait()

    rw = rw_ref[...]
    ffn = rw[:, 0:1] * _load_row_tiles(ybuf.at[slot, 0], tm)
    for k in range(1, TOP_K):
        ffn = ffn + rw[:, k:k + 1] * _load_row_tiles(ybuf.at[slot, k], tm)
    out_ref[...] = _layer_norm(DN_ALPHA * h_ref[...] + ffn, g_ref[...], b_ref[...])


def _combine(pos, ys, h1, rw, g, b):
    n = h1.shape[0]
    tm = CMB_TM
    grid_spec = pltpu.PrefetchScalarGridSpec(
        num_scalar_prefetch=1,
        grid=(n // tm,),
        in_specs=[pl.BlockSpec(memory_space=pl.ANY),
                  pl.BlockSpec((tm, D_MODEL), lambda i, pos: (i, 0)),
                  pl.BlockSpec((tm, LANES), lambda i, pos: (i, 0)),
                  pl.BlockSpec((1, D_MODEL), lambda i, pos: (0, 0)),
                  pl.BlockSpec((1, D_MODEL), lambda i, pos: (0, 0))],
        out_specs=pl.BlockSpec((tm, D_MODEL), lambda i, pos: (i, 0)),
        scratch_shapes=[pltpu.VMEM((2, TOP_K, tm * ROW_TILE, LANES), F32),
                        pltpu.SemaphoreType.DMA((2,))],
    )
    return pl.pallas_call(
        _combine_kernel,
        grid_spec=grid_spec,
        out_shape=jax.ShapeDtypeStruct((n, D_MODEL), F32),
        compiler_params=_cparams(("arbitrary",)),
        name="combine",
    )(pos, ys, h1, rw, g, b)


def _cos_sin_table(pos):
    inv = ROPE_THETA ** (-jnp.arange(0, ROPE_DIM, 2, dtype=F32) / ROPE_DIM)
    ang = pos.astype(F32)[..., None] * inv
    pad = jnp.zeros(ang.shape[:-1] + (LANES - ROPE_DIM,), F32)
    return jnp.concatenate([jnp.cos(ang), jnp.sin(ang), pad], axis=-1)


def _expand_cmp_weights(w1, w2, pe):
    g = ATT_KV_GROUPS
    eye = jnp.eye(g, dtype=F32)
    w1r = w1.reshape(L_CMP, ATT_HEAD_DIM, CMP_HIDDEN)
    w1x = jnp.einsum('ldh,ab->ladbh', w1r, eye).reshape(L_CMP, g * ATT_HEAD_DIM, g * CMP_HIDDEN)
    half = STRIDE_CMP
    top = w1x[:half].reshape(half * g * ATT_HEAD_DIM, g * CMP_HIDDEN).astype(BF16)
    bot = w1x[half:].reshape(half * g * ATT_HEAD_DIM, g * CMP_HIDDEN).astype(BF16)
    w2x = jnp.einsum('hd,ab->ahbd', w2, eye).reshape(g * CMP_HIDDEN, g * ATT_HEAD_DIM).astype(BF16)
    pex = jnp.broadcast_to(pe[:, None, :], (L_CMP, g, ATT_HEAD_DIM))
    pet = pex[:half].reshape(1, half * g * ATT_HEAD_DIM)
    peb = pex[half:].reshape(1, half * g * ATT_HEAD_DIM)
    return top, bot, w2x, pet, peb


def _overlap_matrix(nc_rows, seq):
    n_cmp = (seq - L_CMP) // STRIDE_CMP + 1
    cs = np.arange(nc_rows) * STRIDE_CMP
    ss = np.arange(LANES) * L_SEL
    ov = ((cs[:, None] < ss[None, :] + L_SEL) & (ss[None, :] < cs[:, None] + L_CMP)
          & (np.arange(nc_rows)[:, None] < n_cmp) & (ss[None, :] < seq))
    return jnp.asarray(ov.astype(np.float32), dtype=BF16)


def _block_mask_keys(seq):
    hot = (np.arange(seq)[:, None] // L_SEL) == np.arange(LANES)[None, :]
    return jnp.asarray(np.where(hot, NEG, 0.0).astype(np.float32), dtype=BF16)


def _route_tables(top_i, rank, counts, nblk, bm):
    experts = jnp.arange(N_EXPERTS, dtype=jnp.int32)
    padded = (counts + bm - 1) // bm * bm
    pends = jnp.cumsum(padded)
    pstarts = pends - padded
    onehot = top_i[:, :, None] == experts[None, None, :]
    pos = rank + jnp.sum(jnp.where(onehot, pstarts[None, None, :], 0), axis=-1)
    blk_start = jnp.arange(nblk, dtype=jnp.int32) * bm
    blk_expert = jnp.minimum(jnp.sum((pends[None, :] <= blk_start[:, None]).astype(jnp.int32), axis=1),
                             N_EXPERTS - 1)
    n_used = (pends[-1] // bm).reshape(1)
    n_rows_pad = nblk * bm
    starts = jnp.concatenate([pstarts + counts, pends[-1:]])
    lens = jnp.concatenate([padded - counts, n_rows_pad - pends[-1:]])
    pad_tab = jnp.concatenate([starts, lens, jnp.sum(lens).reshape(1)])
    return pos.reshape(-1).astype(jnp.int32), blk_expert, n_used.astype(jnp.int32), pad_tab.astype(jnp.int32)


def kernel(x, positions, w_in, pe_cmp, w_ck1, w_ck2, w_cv1, w_cv2, hg_lb, hg_norm_g, w_o, ln1_g,
           ln1_b, w_router, b_router, w_gate_up, b_gate_up, w_down, b_down, ln2_g, ln2_b):
    batch, seq, d = x.shape
    n = batch * seq
    assert d == D_MODEL and seq % SEL_TK_BULK == 0 and seq % HG_TC == 0 and n % IN_TM == 0
    assert seq // L_SEL <= LANES and seq >= WINDOW + Q_BLOCK
    l = 0
    x2 = x.reshape(n, d)

    n_att = ATT_WIDTH + 6 * KV_WIDTH
    w_pad = jnp.concatenate(
        [w_in[l][:, :n_att + N_GATE], jnp.zeros((d, LANES - N_GATE), F32), w_in[l][:, n_att + N_GATE:]],
        axis=1).astype(BF16)
    wkt, wkb, wk2, pet, peb = _expand_cmp_weights(w_ck1[l], w_ck2[l], pe_cmp[l])
    wvt, wvb, wv2, _, _ = _expand_cmp_weights(w_cv1[l], w_cv2[l], pe_cmp[l])
    lbs = jnp.cumsum(jax.nn.softmax(hg_lb.astype(F32), axis=0), axis=0)[l].reshape(1, HG_HEADS * HG_DK)
    wr_pad = jnp.concatenate([w_router[l], jnp.zeros((d, LANES - N_EXPERTS), F32)], axis=1)
    wr3 = jnp.concatenate(_split3(wr_pad)[:2], axis=1)
    br_pad = jnp.concatenate([b_router[l], jnp.full((LANES - N_EXPERTS,), NEG, F32)]).reshape(1, LANES)

    cs_tab = _cos_sin_table(positions.reshape(n))
    nc_rows = seq // STRIDE_CMP
    cmp_end = jnp.minimum(jnp.arange(nc_rows) * STRIDE_CMP + L_CMP - 1, seq - 1)
    cs_cmp = _cos_sin_table(positions[:, cmp_end])

    qx, kcmp, vcmp, kva, gate, hg = _in_proj(x2, w_pad, cs_tab)
    kc, vc = _compress(kcmp.reshape(batch, nc_rows, STRIDE_CMP * LANES),
                       vcmp.reshape(batch, nc_rows, STRIDE_CMP * LANES),
                       pet, peb, wkt, wkb, wvt, wvb, wk2, wv2, cs_cmp)
    att = _nsa(qx, gate, kc, vc, kva, _overlap_matrix(nc_rows, seq), _block_mask_keys(seq),
               batch, seq)
    rec = _hgrn2(hg, lbs, hg_norm_g[l].reshape(1, HG_DV), batch, seq)
    h1, h1t, ri, rw = _out_proj(att, rec, x2, w_o[l].astype(BF16), ln1_g[l].reshape(1, d),
                                ln1_b[l].reshape(1, d), wr3, br_pad)

    bm = MOE_BM
    a = n * TOP_K
    nblk = -(-(a + N_EXPERTS * (bm - 1)) // bm)
    rank, counts = _route(ri)
    pos, blk_expert, n_used, pad_tab = _route_tables(
        ri[:, :TOP_K], rank[:, :TOP_K], counts[0, :N_EXPERTS].astype(jnp.int32), nblk, bm)
    xs = _dispatch(pos, pad_tab, h1t, nblk * bm)
    ys = _experts(blk_expert, n_used, xs, w_gate_up[l], b_gate_up[l].reshape(N_EXPERTS, 1, 2 * D_FF),
                  w_down[l], b_down[l].reshape(N_EXPERTS, 1, d))
    out = _combine(pos, ys, h1, rw, ln2_g[l].reshape(1, d), ln2_b[l].reshape(1, d))
    return out.reshape(batch, seq, d)
```

```python
import functools

import numpy as np
import jax
import jax.numpy as jnp
from jax import lax
from jax.experimental import pallas as pl
from jax.experimental.pallas import tpu as pltpu

F32 = jnp.float32
BF16 = jnp.bfloat16

D_MODEL = 1024
ATT_HEADS = 8
ATT_HEAD_DIM = 64
ATT_KV_GROUPS = 2
ATT_HPG = ATT_HEADS // ATT_KV_GROUPS
ATT_WIDTH = ATT_HEADS * ATT_HEAD_DIM
KV_WIDTH = ATT_KV_GROUPS * ATT_HEAD_DIM
N_BRANCH = 3
L_CMP = 32
STRIDE_CMP = 16
CMP_HIDDEN = 256
L_SEL = 64
N_SELECT = 16
N_FORCED_LOCAL = 2
FORCE_BONUS = 1000.0
WINDOW = 512
Q_BLOCK = 128
ROPE_THETA = 500000.0
ROPE_DIM = ATT_HEAD_DIM // 4
ROPE_HALF = ROPE_DIM // 2
HG_HEADS = 4
HG_DK = 128
HG_DV = 128
HG_WIDTH = HG_HEADS * HG_DV
HG_CHUNK = 64
HG_SUB = 16
HG_SAFE_DECAY = 60.0
N_EXPERTS = 32
TOP_K = 4
D_FF = 1024
SWIGLU_LIMIT = 7.0
SWIGLU_ALPHA = 1.702
DEPTH = 1
DN_ALPHA = (2 * DEPTH) ** 0.25
LN_EPS = 1e-5
RMS_EPS = 1e-6
NEG = -1e30
LOG2E = 1.4426950408889634
SEL_SHIFT = L_SEL.bit_length() - 1
SUB_SHIFT = HG_SUB.bit_length() - 1

LANES = 128
VMEM_LIMIT = 56 * 1024 * 1024
IN_TM = 512
SEL_TK = 512
SEL_TK_BULK = 1024
HG_TC = 1024
MOE_BM = 256
CMB_TM = 256
GATHER_UNROLL = 8
ROW_TILE = D_MODEL // LANES
ROUTE_TM = 512

C_Q = 0
C_KCMP = 512
C_VCMP = 640
C_KVA = 768
C_GATE = 1280
C_HG = 1408
IN_COLS = C_HG + 4 * HG_WIDTH
N_GATE = ATT_HEADS * N_BRANCH


def _nt(a, b):
    return lax.dot_general(a, b, (((1,), (1,)), ((), ())), preferred_element_type=F32)


def _tn(a, b):
    return lax.dot_general(a, b, (((0,), (0,)), ((), ())), preferred_element_type=F32)


def _mm(a, b):
    return jnp.dot(a, b, preferred_element_type=F32)


def _split3(x):
    hi = x.astype(BF16)
    r1 = x - hi.astype(F32)
    mid = r1.astype(BF16)
    lo = (r1 - mid.astype(F32)).astype(BF16)
    return hi, mid, lo


def _rope(v, c, s1, s2):
    return v * c + pltpu.roll(v, LANES - ROPE_HALF, 1) * s1 + pltpu.roll(v, ROPE_HALF, 1) * s2


def _cparams(sem):
    return pltpu.CompilerParams(dimension_semantics=sem, vmem_limit_bytes=VMEM_LIMIT)


def _rope_lane_tables(cs):
    lane = lax.broadcasted_iota(jnp.int32, (1, LANES), 1)
    at = lambda d0: (lane >= d0) & (lane < d0 + ROPE_HALF)
    h1 = ATT_HEAD_DIM
    from_lane0 = lambda shift: pltpu.roll(cs, shift % LANES, 1)
    r_h1 = from_lane0(h1)
    c = jnp.where(at(0), cs, jnp.where(at(ROPE_HALF), from_lane0(ROPE_HALF),
        jnp.where(at(h1), r_h1, jnp.where(at(h1 + ROPE_HALF), from_lane0(h1 + ROPE_HALF), 1.0))))
    s1 = jnp.where(at(0), -from_lane0(-ROPE_HALF), jnp.where(at(h1), -from_lane0(h1 - ROPE_HALF), 0.0))
    s2 = jnp.where(at(ROPE_HALF), cs, jnp.where(at(h1 + ROPE_HALF), r_h1, 0.0))
    return c, s1, s2


def _in_proj_kernel(x_ref, w_ref, cs_ref,
                    qx_ref, kcmp_ref, vcmp_ref, kva_ref, gate_ref, hg_ref):
    xb = x_ref[...].astype(BF16)
    c, s1, s2 = _rope_lane_tables(cs_ref[...])
    lane = lax.broadcasted_iota(jnp.int32, (1, LANES), 1)
    low = lane < ATT_HEAD_DIM

    q = _mm(xb, w_ref[:, C_Q:C_Q + ATT_WIDTH])
    scale = ATT_HEAD_DIM ** -0.5 * LOG2E
    for j in range(ATT_WIDTH // LANES):
        v = _rope(q[:, j * LANES:(j + 1) * LANES], c, s1, s2) * scale
        vr = pltpu.roll(v, ATT_HEAD_DIM, 1)
        g = (2 * j) // ATT_HPG
        keep = low if g == 0 else jnp.logical_not(low)
        h0 = jnp.where(keep, v if g == 0 else vr, 0.0)
        h1 = jnp.where(keep, vr if g == 0 else v, 0.0)
        qx_ref[:, (2 * j) * LANES:(2 * j + 1) * LANES] = h0.astype(BF16)
        qx_ref[:, (2 * j + 1) * LANES:(2 * j + 2) * LANES] = h1.astype(BF16)

    kv = _mm(xb, w_ref[:, C_KCMP:C_GATE])
    kcmp_ref[...] = kv[:, 0:128]
    vcmp_ref[...] = kv[:, 128:256]
    kva_ref[:, 0:128] = _rope(kv[:, 256:384], c, s1, s2).astype(BF16)
    kva_ref[:, 128:256] = kv[:, 384:512].astype(BF16)
    kva_ref[:, 256:384] = _rope(kv[:, 512:640], c, s1, s2).astype(BF16)
    kva_ref[:, 384:512] = kv[:, 640:768].astype(BF16)

    gate_ref[...] = jax.nn.sigmoid(_mm(xb, w_ref[:, C_GATE:C_HG]))
    for j in range(4):
        hg_ref[:, j * HG_WIDTH:(j + 1) * HG_WIDTH] = _mm(
            xb, w_ref[:, C_HG + j * HG_WIDTH:C_HG + (j + 1) * HG_WIDTH])


def _in_proj(x2, w_pad, cs_tab):
    n = x2.shape[0]
    tm = IN_TM
    row = lambda w: pl.BlockSpec((tm, w), lambda i: (i, 0))
    return pl.pallas_call(
        _in_proj_kernel,
        grid=(n // tm,),
        in_specs=[row(D_MODEL), pl.BlockSpec((D_MODEL, IN_COLS), lambda i: (0, 0)),
                  row(LANES)],
        out_specs=[row(ATT_HEADS * LANES), row(LANES), row(LANES), row(4 * LANES), row(LANES),
                   row(4 * HG_WIDTH)],
        out_shape=[jax.ShapeDtypeStruct((n, ATT_HEADS * LANES), BF16),
                   jax.ShapeDtypeStruct((n, LANES), F32),
                   jax.ShapeDtypeStruct((n, LANES), F32),
                   jax.ShapeDtypeStruct((n, 4 * LANES), BF16),
                   jax.ShapeDtypeStruct((n, LANES), F32),
                   jax.ShapeDtypeStruct((n, 4 * HG_WIDTH), F32)],
        compiler_params=_cparams(("arbitrary",)),
        name="in_proj",
    )(x2, w_pad, cs_tab)


def _compress_kernel(kf_ref, vf_ref, pet_ref, peb_ref, wkt_ref, wkb_ref, wvt_ref, wvb_ref,
                     wk2_ref, wv2_ref, cs_ref, kc_ref, vc_ref):
    nc = kf_ref.shape[0]

    def mlp(x, wt, wb, w2):
        top = (x + pet_ref[...]).astype(BF16)
        bot = (x + peb_ref[...]).astype(BF16)
        u = _mm(top, wt[...])
        v = _mm(bot, wb[...])
        h = u + pltpu.roll(v, nc - 1, 0)
        return _mm(jax.nn.gelu(h).astype(BF16), w2[...])

    kc = mlp(kf_ref[...], wkt_ref, wkb_ref, wk2_ref)
    kc_ref[...] = _rope(kc, *_rope_lane_tables(cs_ref[...])).astype(BF16)
    vc_ref[...] = mlp(vf_ref[...], wvt_ref, wvb_ref, wv2_ref).astype(BF16)


def _compress(kf, vf, pet, peb, wkt, wkb, wvt, wvb, wk2, wv2, cs_cmp):
    b, nc, w = kf.shape
    per_b = lambda width: pl.BlockSpec((None, nc, width), lambda i: (i, 0, 0))
    full = lambda a: pl.BlockSpec(a.shape, lambda i: (0,) * a.ndim)
    return pl.pallas_call(
        _compress_kernel,
        grid=(b,),
        in_specs=[per_b(w), per_b(w), full(pet), full(peb), full(wkt), full(wkb), full(wvt),
                  full(wvb), full(wk2), full(wv2), per_b(LANES)],
        out_specs=[per_b(LANES), per_b(LANES)],
        out_shape=[jax.ShapeDtypeStruct((b, nc, LANES), BF16)] * 2,
        compiler_params=_cparams(("arbitrary",)),
        name="compress",
    )(kf, vf, pet, peb, wkt, wkb, wvt, wvb, wk2, wv2, cs_cmp)


def _nsa_kernel(qx_ref, gate_ref, kc_ref, vc_ref, kva_ref, ovl_ref, ene_ref, out_ref,
                m_ref, acc_ref, *, seq):
    t = Q_BLOCK
    rows = ATT_HPG * t
    qb = pl.program_id(1)
    s0 = qb * t
    t_pos = s0 + lax.broadcasted_iota(jnp.int32, (t, 1), 0)
    nc = kc_ref.shape[0]
    gates = gate_ref[...]
    lane = lax.broadcasted_iota(jnp.int32, (1, LANES), 1)

    def tile4(a):
        return jnp.concatenate([a] * ATT_HPG, axis=0)

    def per_head(fn, s):
        return jnp.concatenate([fn(s[n * t:(n + 1) * t]) for n in range(ATT_HEADS)], axis=0)

    def softmax_num(s):
        return jnp.exp2(s - jnp.max(s, axis=-1, keepdims=True)).astype(BF16)

    cmp_end = lax.broadcasted_iota(jnp.int32, (1, nc), 1) * STRIDE_CMP + (L_CMP - 1)
    bias_c = jnp.where((cmp_end <= t_pos) & (cmp_end < seq), 0.0, NEG)
    has_cmp = tile4(t_pos >= L_CMP - 1)
    wlen = WINDOW + t
    ws = pl.multiple_of(jnp.maximum(s0 - WINDOW, 0), t)
    kp = ws + lax.broadcasted_iota(jnp.int32, (1, wlen), 1)
    bias_w = jnp.where((kp <= t_pos) & (kp > t_pos - WINDOW), 0.0, NEG)
    dist = jnp.right_shift(t_pos, SEL_SHIFT) - lane
    causal_blk = dist >= 0
    bonus = jnp.where((lane == 0) | (causal_blk & (dist < N_FORCED_LOCAL)), FORCE_BONUS, 0.0)
    brow = lax.broadcasted_iota(jnp.int32, (LANES, ATT_KV_GROUPS * t), 0)

    def with_ones(v, g):
        in_g = (lane >= g * ATT_HEAD_DIM) & (lane < (g + 1) * ATT_HEAD_DIM)
        return v * jnp.where(in_g, 1.0, 0.0).astype(BF16) + jnp.where(in_g, 0.0, 1.0).astype(BF16)

    groups = range(ATT_KV_GROUPS)
    den = [(1 - g) * ATT_HEAD_DIM for g in groups]
    q_all = jnp.concatenate([qx_ref[:, h * LANES:(h + 1) * LANES] for h in range(ATT_HEADS)],
                            axis=0)

    p_c = softmax_num(per_head(lambda a: a + bias_c, _nt(q_all, kc_ref[...])))
    o_cs, imps = [], []
    for g in groups:
        oi = _mm(p_c[g * rows:(g + 1) * rows],
                 jnp.concatenate([with_ones(vc_ref[...], g), ovl_ref[...]], axis=1))
        inv_c = jnp.where(has_cmp, 1.0 / jnp.maximum(oi[:, den[g]:den[g] + 1], 1e-20), 0.0)
        o_cs.append(oi[:, :LANES] * inv_c)
        imp4 = oi[:, LANES:] * inv_c
        imp = imp4[0:t]
        for n in range(1, ATT_HPG):
            imp = imp + imp4[n * t:(n + 1) * t]
        imps.append(jnp.where(causal_blk, imp + bonus, -1.0))

    imp_t = jnp.concatenate(imps, axis=0).T
    sel_t = jnp.zeros(imp_t.shape, F32)
    for _ in range(N_SELECT):
        mx = jnp.max(imp_t, axis=0, keepdims=True)
        idx = jnp.min(jnp.where(imp_t == mx, brow, LANES), axis=0, keepdims=True)
        pick = brow == idx
        sel_t = jnp.where(pick, 1.0, sel_t)
        imp_t = jnp.where(pick, -3e38, imp_t)
    sel = sel_t.T
    unsel = [jnp.where(causal_blk, 1.0 - sel[g * t:(g + 1) * t], 1.0).astype(BF16) for g in groups]

    lhs = jnp.concatenate([q_all, jnp.concatenate([tile4(u) for u in unsel], axis=0)],
                          axis=1)
    m_ref[...] = jnp.full(m_ref.shape, NEG, F32)
    acc_ref[...] = jnp.zeros(acc_ref.shape, F32)

    def sel_tile(k0, tk, causal):
        rhs = jnp.concatenate([kva_ref[pl.ds(k0, tk), 0:128], ene_ref[pl.ds(k0, tk), :]],
                              axis=1)
        v_t = kva_ref[pl.ds(k0, tk), 128:256]
        s = _nt(lhs, rhs)
        if causal:
            future = k0 + lax.broadcasted_iota(jnp.int32, (1, tk), 1) > t_pos
            s = per_head(lambda a: jnp.where(future, NEG, a), s)
        m_old = m_ref[...]
        m_new = jnp.maximum(m_old, jnp.max(s, axis=-1, keepdims=True))
        m_ref[...] = m_new
        p = jnp.concatenate([s[:, j * LANES:(j + 1) * LANES] - m_new
                             for j in range(tk // LANES)], axis=1)
        p = jnp.exp2(p).astype(BF16)
        pv = jnp.concatenate([_mm(p[g * rows:(g + 1) * rows], with_ones(v_t, g)) for g in groups],
                             axis=0)
        acc_ref[...] = jnp.exp2(m_old - m_new) * acc_ref[...] + pv

    n_bulk = s0 // SEL_TK_BULK
    n_tail = (s0 + t - n_bulk * SEL_TK_BULK + SEL_TK - 1) // SEL_TK

    def bulk_tile(kt, carry):
        sel_tile(pl.multiple_of(kt * SEL_TK_BULK, SEL_TK_BULK), SEL_TK_BULK, False)
        return carry

    def tail_tile(kt, carry):
        sel_tile(pl.multiple_of(n_bulk * SEL_TK_BULK + kt * SEL_TK, SEL_TK), SEL_TK, True)
        return carry

    lax.fori_loop(0, n_bulk, bulk_tile, 0)
    lax.fori_loop(0, n_tail, tail_tile, 0)

    p_w = softmax_num(per_head(lambda a: a + bias_w,
                               _nt(q_all, kva_ref[pl.ds(ws, wlen), 256:384])))

    pieces = []
    for g in groups:
        acc = acc_ref[g * rows:(g + 1) * rows, :]
        o_s = acc * (1.0 / jnp.maximum(acc[:, den[g]:den[g] + 1], 1e-20))
        o_w = _mm(p_w[g * rows:(g + 1) * rows], with_ones(kva_ref[pl.ds(ws, wlen), 384:512], g))
        o_w = o_w * (1.0 / jnp.maximum(o_w[:, den[g]:den[g] + 1], 1e-20))

        o_c = o_cs[g]
        for n in range(ATT_HPG):
            c0 = (g * ATT_HPG + n) * N_BRANCH
            r = slice(n * t, (n + 1) * t)
            ln = slice(g * ATT_HEAD_DIM, (g + 1) * ATT_HEAD_DIM)
            pieces.append(gates[:, c0:c0 + 1] * o_c[r, ln]
                          + gates[:, c0 + 1:c0 + 2] * o_s[r, ln]
                          + gates[:, c0 + 2:c0 + 3] * o_w[r, ln])
    out_ref[...] = jnp.concatenate(pieces, axis=1).astype(BF16)


def _nsa(qx, gate, kc, vc, kva, ovl, ene, batch, seq):
    n = qx.shape[0]
    nqb = seq // Q_BLOCK
    nc = kc.shape[1]
    row = lambda w: pl.BlockSpec((Q_BLOCK, w), lambda b, i: (b * nqb + i, 0))
    return pl.pallas_call(
        functools.partial(_nsa_kernel, seq=seq),
        grid=(batch, nqb),
        in_specs=[row(ATT_HEADS * LANES), row(LANES),
                  pl.BlockSpec((None, nc, LANES), lambda b, i: (b, 0, 0)),
                  pl.BlockSpec((None, nc, LANES), lambda b, i: (b, 0, 0)),
                  pl.BlockSpec((None, seq, 4 * LANES), lambda b, i: (b, 0, 0)),
                  pl.BlockSpec(ovl.shape, lambda b, i: (0, 0)),
                  pl.BlockSpec(ene.shape, lambda b, i: (0, 0))],
        out_specs=row(ATT_WIDTH),
        out_shape=jax.ShapeDtypeStruct((n, ATT_WIDTH), BF16),
        scratch_shapes=[pltpu.VMEM((ATT_HEADS * Q_BLOCK, LANES), F32),
                        pltpu.VMEM((ATT_HEADS * Q_BLOCK, LANES), F32)],
        compiler_params=_cparams(("arbitrary", "arbitrary")),
        name="nsa",
    )(qx, gate, kc, vc, kva.reshape(batch, seq, 4 * LANES), ovl, ene)


def _bmm(a, b):
    return lax.dot_general(a, b, (((2,), (1,)), ((0,), (0,))), preferred_element_type=F32)


def _bnt(a, b):
    return lax.dot_general(a, b, (((2,), (2,)), ((0,), (0,))), preferred_element_type=F32)


def _hgrn2_kernel(q_ref, f_ref, i_ref, g_ref, lb_ref, ng_ref, out_ref, st_ref, b_ref, kk_ref):
    c = HG_CHUNK
    sub = HG_SUB
    nsub = c // sub
    tc = q_ref.shape[0]
    nch = tc // c

    @pl.when(pl.program_id(2) == 0)
    def _():
        st_ref[...] = jnp.zeros_like(st_ref)

    lb = lb_ref[...]
    f_hat = jax.nn.sigmoid(f_ref[...])
    logf = jnp.log(lb + (1.0 - lb) * f_hat)
    kk_ref[...] = (1.0 - lb) * (1.0 - f_hat)

    ri = lax.broadcasted_iota(jnp.int32, (c, c), 0)
    ci = lax.broadcasted_iota(jnp.int32, (c, c), 1)
    lower = ci <= ri
    ng = ng_ref[...]

    tril3 = jnp.broadcast_to(jnp.where(lower, 1.0, 0.0).astype(BF16)[None], (nch, c, c))
    b3 = None
    for part in _split3(logf):
        term = _bmm(tril3, part.reshape(nch, c, HG_DK))
        b3 = term if b3 is None else b3 + term
    b_ref[...] = b3.reshape(tc, HG_DK)
    worst_decay = jnp.max(-b3[:, c - 1:c, :])

    def finish(o, rows):
        o = o * lax.rsqrt(jnp.mean(o * o, axis=-1, keepdims=True) + RMS_EPS) * ng
        out_ref[rows, :] = (o * jax.nn.silu(g_ref[rows, :])).astype(BF16)

    def whole_chunk_decay():
        b = b_ref[...].reshape(nch, c, HG_DK)
        eb = jnp.exp(b)
        qe = (q_ref[...].reshape(nch, c, HG_DK) * eb).astype(BF16)
        ke = kk_ref[...].reshape(nch, c, HG_DK) * jnp.exp(-b)
        v = i_ref[...].astype(BF16).reshape(nch, c, HG_DV)
        a = jnp.where(lower[None], _bnt(qe, ke.astype(BF16)), 0.0).astype(BF16)
        o_intra = _bmm(a, v)
        eb_last = eb[:, c - 1:c, :]
        kd = (ke * eb_last).astype(BF16)
        upd = lax.dot_general(v, kd, (((1,), (1,)), ((0,), (0,))), preferred_element_type=F32)
        st = st_ref[...]
        sts = []
        for ic in range(nch):
            sts.append(st.astype(BF16))
            st = st * eb_last[ic] + upd[ic]
        st_ref[...] = st
        o = o_intra + _bnt(qe, jnp.stack(sts))
        finish(o.reshape(tc, HG_DV), slice(None))

    def sub_block_decay():
        lax.fori_loop(0, nch, chunk, 0)

    row = lax.broadcasted_iota(jnp.int32, (c, 1), 0)
    row_sub = jnp.right_shift(row, SUB_SHIFT)
    row_loc = row - row_sub * sub

    def chunk(ic, carry):
        r0 = pl.multiple_of(ic * c, c)
        q = q_ref[pl.ds(r0, c), :]
        k = kk_ref[pl.ds(r0, c), :]
        v = i_ref[pl.ds(r0, c), :].astype(BF16)
        b = b_ref[pl.ds(r0, c), :]
        st = st_ref[...]
        o = _nt((q * jnp.exp(b)).astype(BF16), st.astype(BF16))

        a_rows = [jnp.zeros((sub, c), F32)]
        for i in range(1, nsub):
            ref_b = b[i * sub:i * sub + 1, :]
            qs = q[i * sub:(i + 1) * sub, :] * jnp.exp(b[i * sub:(i + 1) * sub, :] - ref_b)
            ks = jnp.where(row < i * sub, k * jnp.exp(jnp.minimum(ref_b - b, 0.0)), 0.0)
            a_rows.append(_nt(qs.astype(BF16), ks.astype(BF16)))
        a = jnp.concatenate(a_rows, axis=0)

        b3 = b.reshape(nsub, sub, HG_DK)
        q3 = q.reshape(nsub, sub, HG_DK)
        k3 = k.reshape(nsub, sub, HG_DK)
        for s in range(sub):
            dec = jnp.exp(jnp.minimum(b3 - b3[:, s:s + 1, :], 0.0))
            w = (q3 * k3[:, s:s + 1, :] * dec).reshape(c, HG_DK)
            col = jnp.sum(w, axis=-1, keepdims=True)
            hit = (ci == row_sub * sub + s) & (row_loc >= s)
            a = a + jnp.where(hit, col, 0.0)

        o = o + _mm(a.astype(BF16), v)
        b_last = b[c - 1:c, :]
        kd = (k * jnp.exp(b_last - b)).astype(BF16)
        st_ref[...] = st * jnp.exp(b_last) + _tn(v, kd)
        finish(o, pl.ds(r0, c))
        return carry

    lax.cond(worst_decay < HG_SAFE_DECAY, whole_chunk_decay, sub_block_decay)


def _hgrn2(hg, lb, ng, batch, seq):
    n = hg.shape[0]
    tc = HG_TC
    nt = seq // tc
    col = lambda j: pl.BlockSpec((tc, HG_DK), lambda b, h, i, j=j: (b * nt + i, j * HG_HEADS + h))
    return pl.pallas_call(
        _hgrn2_kernel,
        grid=(batch, HG_HEADS, nt),
        in_specs=[col(0), col(1), col(2), col(3),
                  pl.BlockSpec((1, HG_DK), lambda b, h, i: (0, h)),
                  pl.BlockSpec((1, HG_DV), lambda b, h, i: (0, 0))],
        out_specs=pl.BlockSpec((tc, HG_DV), lambda b, h, i: (b * nt + i, h)),
        out_shape=jax.ShapeDtypeStruct((n, HG_WIDTH), BF16),
        scratch_shapes=[pltpu.VMEM((HG_DV, HG_DK), F32),
                        pltpu.VMEM((tc, HG_DK), F32),
                        pltpu.VMEM((tc, HG_DK), F32)],
        compiler_params=_cparams(("arbitrary", "arbitrary", "arbitrary")),
        name="hgrn2",
    )(hg, hg, hg, hg, lb, ng)


def _store_row_tiles(ref, v):
    rows = v.shape[0]
    for s in range(ROW_TILE):
        ref[pl.ds(s, rows, stride=ROW_TILE), :] = v[:, s * LANES:(s + 1) * LANES]


def _load_row_tiles(ref, rows):
    return jnp.concatenate([ref[pl.ds(s, rows, stride=ROW_TILE), :] for s in range(ROW_TILE)],
                           axis=1)


def _layer_norm(y, g, b):
    mu = jnp.mean(y, axis=-1, keepdims=True)
    d = y - mu
    var = jnp.mean(d * d, axis=-1, keepdims=True)
    return d * lax.rsqrt(var + LN_EPS) * g + b


def _out_proj_kernel(att_ref, rec_ref, x_ref, wo_ref, g_ref, b_ref, wr_ref, br_ref,
                     ht_ref, ri_ref, rw_ref):
    mix = _mm(att_ref[...], wo_ref[0:ATT_WIDTH, :]) + _mm(rec_ref[...], wo_ref[ATT_WIDTH:, :])
    h = _layer_norm(DN_ALPHA * x_ref[...] + mix, g_ref[...], b_ref[...])
    _store_row_tiles(ht_ref, h)

    hh = h.astype(BF16)
    hm = (h - hh.astype(F32)).astype(BF16)
    hw = _mm(hh, wr_ref[...])
    lg = (hw[:, :LANES] + hw[:, LANES:] + _mm(hm, wr_ref[:, :LANES])
          + br_ref[...])
    lane = lax.broadcasted_iota(jnp.int32, lg.shape, 1)
    ri = jnp.zeros(lg.shape, jnp.int32)
    rw = jnp.zeros(lg.shape, F32)
    v0 = None
    for r in range(TOP_K):
        mx = jnp.max(lg, axis=-1, keepdims=True)
        idx = jnp.min(jnp.where(lg == mx, lane, LANES), axis=-1, keepdims=True)
        if r == 0:
            v0 = mx
        ri = jnp.where(lane == r, idx, ri)
        rw = jnp.where(lane == r, jnp.exp(mx - v0), rw)
        lg = jnp.where(lane == idx, -3e38, lg)
    ri_ref[...] = ri
    rw_ref[...] = rw / jnp.sum(rw, axis=-1, keepdims=True)


def _out_proj(att, rec, x2, wo, g, b, wr3, br):
    n = x2.shape[0]
    tm = IN_TM
    row = lambda w: pl.BlockSpec((tm, w), lambda i: (i, 0))
    full = lambda a: pl.BlockSpec(a.shape, lambda i: (0,) * a.ndim)
    return pl.pallas_call(
        _out_proj_kernel,
        grid=(n // tm,),
        in_specs=[row(ATT_WIDTH), row(HG_WIDTH), row(D_MODEL), full(wo), full(g), full(b),
                  full(wr3), full(br)],
        out_specs=[pl.BlockSpec((tm * ROW_TILE, LANES), lambda i: (i, 0)), row(LANES), row(LANES)],
        out_shape=[jax.ShapeDtypeStruct((n * ROW_TILE, LANES), F32),
                   jax.ShapeDtypeStruct((n, LANES), jnp.int32),
                   jax.ShapeDtypeStruct((n, LANES), F32)],
        compiler_params=_cparams(("arbitrary",)),
        name="out_proj",
    )(att, rec, x2, wo, g, b, wr3, br)


def _route_kernel(ri_ref, rank_ref, cnt_ref, carry_ref):
    tm = ri_ref.shape[0]

    @pl.when(pl.program_id(0) == 0)
    def _():
        carry_ref[...] = jnp.zeros_like(carry_ref)

    ri = ri_ref[...]
    lane = lax.broadcasted_iota(jnp.int32, (tm, LANES), 1)
    hot = [lane == ri[:, k:k + 1] for k in range(TOP_K)]
    cnt = jnp.where(hot[0], 1.0, 0.0)
    for k in range(1, TOP_K):
        cnt = cnt + jnp.where(hot[k], 1.0, 0.0)
    tri = jnp.where(lax.broadcasted_iota(jnp.int32, (tm, tm), 1)
                    < lax.broadcasted_iota(jnp.int32, (tm, tm), 0), 1.0, 0.0).astype(BF16)
    before = carry_ref[...] + _mm(tri, cnt.astype(BF16))
    rank = jnp.zeros((tm, LANES), jnp.int32)
    for k in range(TOP_K):
        rk = jnp.sum(jnp.where(hot[k], before, 0.0), axis=-1, keepdims=True)
        rank = jnp.where(lane == k, rk.astype(jnp.int32), rank)
    rank_ref[...] = rank
    carry = carry_ref[...] + jnp.sum(cnt, axis=0, keepdims=True)
    carry_ref[...] = carry
    cnt_ref[...] = carry


def _route(ri):
    n = ri.shape[0]
    tm = ROUTE_TM
    return pl.pallas_call(
        _route_kernel,
        grid=(n // tm,),
        in_specs=[pl.BlockSpec((tm, LANES), lambda i: (i, 0))],
        out_specs=[pl.BlockSpec((tm, LANES), lambda i: (i, 0)),
                   pl.BlockSpec((1, LANES), lambda i: (0, 0))],
        out_shape=[jax.ShapeDtypeStruct((n, LANES), jnp.int32),
                   jax.ShapeDtypeStruct((1, LANES), F32)],
        scratch_shapes=[pltpu.VMEM((1, LANES), F32)],
        compiler_params=_cparams(("arbitrary",)),
        name="route",
    )(ri)


def _tile_rows(ref, row):
    return ref.at[pl.ds(pl.multiple_of(row * ROW_TILE, ROW_TILE), ROW_TILE), :]


def _dispatch_kernel(pos_ref, pad_ref, ht_ref, xs_hbm, zero_ref, sem, pad_sem):
    i = pl.program_id(0)
    tm = ht_ref.shape[0] // ROW_TILE

    def issue(r, carry):
        for k in range(TOP_K):
            p = pos_ref[(i * tm + r) * TOP_K + k]
            pltpu.make_async_copy(_tile_rows(ht_ref, r), _tile_rows(xs_hbm, p), sem).start(
                priority=k % 2)
        return carry

    lax.fori_loop(0, tm, issue, 0, unroll=GATHER_UNROLL // TOP_K)

    @pl.when(i == 0)
    def _():
        zero_ref[...] = jnp.zeros_like(zero_ref)
        n_span = (pad_ref.shape[0] - 1) // 2

        def per_span(e, carry):
            start = pad_ref[e]

            def fill(j, c):
                pltpu.make_async_copy(zero_ref, _tile_rows(xs_hbm, start + j), pad_sem).start()
                return c

            return lax.fori_loop(0, pad_ref[n_span + e], fill, carry)

        lax.fori_loop(0, n_span, per_span, 0)

        def drain(j, c):
            pltpu.make_async_copy(zero_ref, _tile_rows(xs_hbm, 0), pad_sem).wait()
            return c

        lax.fori_loop(0, pad_ref[2 * n_span], drain, 0)

    for k in range(TOP_K):
        pltpu.make_async_copy(ht_ref, xs_hbm.at[pl.ds(0, tm * ROW_TILE), :], sem).wait()


def _dispatch(pos, pad_tab, ht, n_rows_pad):
    n = ht.shape[0] // ROW_TILE
    tm = CMB_TM
    grid_spec = pltpu.PrefetchScalarGridSpec(
        num_scalar_prefetch=2,
        grid=(n // tm,),
        in_specs=[pl.BlockSpec((tm * ROW_TILE, LANES), lambda i, pos, pad: (i, 0))],
        out_specs=pl.BlockSpec(memory_space=pl.ANY),
        scratch_shapes=[pltpu.VMEM((ROW_TILE, LANES), F32), pltpu.SemaphoreType.DMA,
                        pltpu.SemaphoreType.DMA],
    )
    return pl.pallas_call(
        _dispatch_kernel,
        grid_spec=grid_spec,
        out_shape=jax.ShapeDtypeStruct((n_rows_pad * ROW_TILE, LANES), F32),
        compiler_params=_cparams(("arbitrary",)),
        name="dispatch",
    )(pos, pad_tab, ht)


def _experts_kernel(be_ref, nu_ref, xs_ref, wgu_ref, bgu_ref, wdn_ref, bdn_ref, y_ref,
                    wgu_bf, wdn_bf):
    i = pl.program_id(0)
    bm = xs_ref.shape[0] // ROW_TILE

    @pl.when(i < nu_ref[0])
    def _():
        @pl.when((i == 0) | (be_ref[i] != be_ref[jnp.maximum(i - 1, 0)]))
        def _():
            def cast_rows(r, carry):
                rows = pl.ds(pl.multiple_of(r * LANES, LANES), LANES)
                wgu_bf[rows, :] = wgu_ref[rows, :].astype(BF16)
                wdn_bf[rows, :] = wdn_ref[rows, :].astype(BF16)
                return carry
            lax.fori_loop(0, D_MODEL // LANES, cast_rows, 0)

        xb = _load_row_tiles(xs_ref, bm).astype(BF16)
        h = _mm(xb, wgu_bf[...]) + bgu_ref[...]
        gate = jnp.minimum(h[:, :D_FF], SWIGLU_LIMIT)
        up = jnp.clip(h[:, D_FF:], -SWIGLU_LIMIT, SWIGLU_LIMIT)
        act = gate * jax.nn.sigmoid(SWIGLU_ALPHA * gate) * (up + 1.0)
        _store_row_tiles(y_ref, _mm(act.astype(BF16), wdn_bf[...]) + bdn_ref[...])

    @pl.when(i >= nu_ref[0])
    def _():
        y_ref[...] = jnp.zeros_like(y_ref)


def _experts(blk_expert, n_used, xs, wgu, bgu, wdn, bdn):
    nblk = blk_expert.shape[0]
    bm = MOE_BM
    used = lambda i, nu: jnp.minimum(i, nu[0] - 1)
    grid_spec = pltpu.PrefetchScalarGridSpec(
        num_scalar_prefetch=2,
        grid=(nblk,),
        in_specs=[pl.BlockSpec((bm * ROW_TILE, LANES), lambda i, be, nu: (used(i, nu), 0)),
                  pl.BlockSpec((None, D_MODEL, 2 * D_FF), lambda i, be, nu: (be[used(i, nu)], 0, 0)),
                  pl.BlockSpec((None, 1, 2 * D_FF), lambda i, be, nu: (be[used(i, nu)], 0, 0)),
                  pl.BlockSpec((None, D_FF, D_MODEL), lambda i, be, nu: (be[used(i, nu)], 0, 0)),
                  pl.BlockSpec((None, 1, D_MODEL), lambda i, be, nu: (be[used(i, nu)], 0, 0))],
        out_specs=pl.BlockSpec((bm * ROW_TILE, LANES), lambda i, be, nu: (i, 0)),
        scratch_shapes=[pltpu.VMEM((D_MODEL, 2 * D_FF), BF16), pltpu.VMEM((D_FF, D_MODEL), BF16)],
    )
    return pl.pallas_call(
        _experts_kernel,
        grid_spec=grid_spec,
        out_shape=jax.ShapeDtypeStruct((nblk * bm * ROW_TILE, LANES), F32),
        compiler_params=_cparams(("arbitrary",)),
        name="experts",
    )(blk_expert, n_used, xs, wgu, bgu, wdn, bdn)


def _combine_kernel(pos_ref, y_hbm, ht_ref, rw_ref, g_ref, b_ref, out_ref, ybuf, sem):
    i = pl.program_id(0)
    nstep = pl.num_programs(0)
    tm = ybuf.shape[2] // ROW_TILE

    def gather(step, slot):
        def issue(r, carry):
            for k in range(TOP_K):
                p = pos_ref[(step * tm + r) * TOP_K + k]
                pltpu.make_async_copy(_tile_rows(y_hbm, p), _tile_rows(ybuf.at[slot, k], r),
                                      sem.at[slot]).start(priority=k % 2)
            return carry
        lax.fori_loop(0, tm, issue, 0, unroll=GATHER_UNROLL // TOP_K)

    @pl.when(i == 0)
    def _():
        gather(0, 0)

    @pl.when(i + 1 < nstep)
    def _():
        gather(i + 1, (i + 1) % 2)

    slot = i % 2
    for k in range(TOP_K):
        pltpu.make_async_copy(y_hbm.at[pl.ds(0, tm * ROW_TILE), :], ybuf.at[slot, k],
                              sem.at[slot]).wait()

    rw = rw_ref[...]
    ffn = rw[:, 0:1] * _load_row_tiles(ybuf.at[slot, 0], tm)
    for k in range(1, TOP_K):
        ffn = ffn + rw[:, k:k + 1] * _load_row_tiles(ybuf.at[slot, k], tm)
    out_ref[...] = _layer_norm(DN_ALPHA * _load_row_tiles(ht_ref, tm) + ffn, g_ref[...], b_ref[...])


def _combine(pos, ys, h1t, rw, g, b):
    n = h1t.shape[0] // ROW_TILE
    tm = CMB_TM
    grid_spec = pltpu.PrefetchScalarGridSpec(
        num_scalar_prefetch=1,
        grid=(n // tm,),
        in_specs=[pl.BlockSpec(memory_space=pl.ANY),
                  pl.BlockSpec((tm * ROW_TILE, LANES), lambda i, pos: (i, 0)),
                  pl.BlockSpec((tm, LANES), lambda i, pos: (i, 0)),
                  pl.BlockSpec((1, D_MODEL), lambda i, pos: (0, 0)),
                  pl.BlockSpec((1, D_MODEL), lambda i, pos: (0, 0))],
        out_specs=pl.BlockSpec((tm, D_MODEL), lambda i, pos: (i, 0)),
        scratch_shapes=[pltpu.VMEM((2, TOP_K, tm * ROW_TILE, LANES), F32),
                        pltpu.SemaphoreType.DMA((2,))],
    )
    return pl.pallas_call(
        _combine_kernel,
        grid_spec=grid_spec,
        out_shape=jax.ShapeDtypeStruct((n, D_MODEL), F32),
        compiler_params=_cparams(("arbitrary",)),
        name="combine",
    )(pos, ys, h1t, rw, g, b)


def _cos_sin_table(pos):
    inv = ROPE_THETA ** (-jnp.arange(0, ROPE_DIM, 2, dtype=F32) / ROPE_DIM)
    ang = pos.astype(F32)[..., None] * inv
    pad = jnp.zeros(ang.shape[:-1] + (LANES - ROPE_DIM,), F32)
    return jnp.concatenate([jnp.cos(ang), jnp.sin(ang), pad], axis=-1)


def _expand_cmp_weights(w1, w2, pe):
    g = ATT_KV_GROUPS
    eye = jnp.eye(g, dtype=F32)
    w1r = w1.reshape(L_CMP, ATT_HEAD_DIM, CMP_HIDDEN)
    w1x = jnp.einsum('ldh,ab->ladbh', w1r, eye).reshape(L_CMP, g * ATT_HEAD_DIM, g * CMP_HIDDEN)
    half = STRIDE_CMP
    top = w1x[:half].reshape(half * g * ATT_HEAD_DIM, g * CMP_HIDDEN).astype(BF16)
    bot = w1x[half:].reshape(half * g * ATT_HEAD_DIM, g * CMP_HIDDEN).astype(BF16)
    w2x = jnp.einsum('hd,ab->ahbd', w2, eye).reshape(g * CMP_HIDDEN, g * ATT_HEAD_DIM).astype(BF16)
    pex = jnp.broadcast_to(pe[:, None, :], (L_CMP, g, ATT_HEAD_DIM))
    pet = pex[:half].reshape(1, half * g * ATT_HEAD_DIM)
    peb = pex[half:].reshape(1, half * g * ATT_HEAD_DIM)
    return top, bot, w2x, pet, peb


def _overlap_matrix(nc_rows, seq):
    n_cmp = (seq - L_CMP) // STRIDE_CMP + 1
    cs = np.arange(nc_rows) * STRIDE_CMP
    ss = np.arange(LANES) * L_SEL
    ov = ((cs[:, None] < ss[None, :] + L_SEL) & (ss[None, :] < cs[:, None] + L_CMP)
          & (np.arange(nc_rows)[:, None] < n_cmp) & (ss[None, :] < seq))
    return jnp.asarray(ov.astype(np.float32), dtype=BF16)


def _block_mask_keys(seq):
    hot = (np.arange(seq)[:, None] // L_SEL) == np.arange(LANES)[None, :]
    return jnp.asarray(np.where(hot, NEG, 0.0).astype(np.float32), dtype=BF16)


def _route_tables(top_i, rank, counts, nblk, bm):
    experts = jnp.arange(N_EXPERTS, dtype=jnp.int32)
    padded = (counts + bm - 1) // bm * bm
    pends = jnp.cumsum(padded)
    pstarts = pends - padded
    onehot = top_i[:, :, None] == experts[None, None, :]
    pos = rank + jnp.sum(jnp.where(onehot, pstarts[None, None, :], 0), axis=-1)
    blk_start = jnp.arange(nblk, dtype=jnp.int32) * bm
    blk_expert = jnp.minimum(jnp.sum((pends[None, :] <= blk_start[:, None]).astype(jnp.int32), axis=1),
                             N_EXPERTS - 1)
    n_used = (pends[-1] // bm).reshape(1)
    n_rows_pad = nblk * bm
    starts = jnp.concatenate([pstarts + counts, pends[-1:]])
    lens = jnp.concatenate([padded - counts, n_rows_pad - pends[-1:]])
    pad_tab = jnp.concatenate([starts, lens, jnp.sum(lens).reshape(1)])
    return pos.reshape(-1).astype(jnp.int32), blk_expert, n_used.astype(jnp.int32), pad_tab.astype(jnp.int32)


def kernel(x, positions, w_in, pe_cmp, w_ck1, w_ck2, w_cv1, w_cv2, hg_lb, hg_norm_g, w_o, ln1_g,
           ln1_b, w_router, b_router, w_gate_up, b_gate_up, w_down, b_down, ln2_g, ln2_b):
    batch, seq, d = x.shape
    n = batch * seq
    assert d == D_MODEL and seq % SEL_TK_BULK == 0 and seq % HG_TC == 0 and n % IN_TM == 0
    assert seq // L_SEL <= LANES and seq >= WINDOW + Q_BLOCK
    l = 0
    x2 = x.reshape(n, d)

    n_att = ATT_WIDTH + 6 * KV_WIDTH
    w_pad = jnp.concatenate(
        [w_in[l][:, :n_att + N_GATE], jnp.zeros((d, LANES - N_GATE), F32), w_in[l][:, n_att + N_GATE:]],
        axis=1).astype(BF16)
    wkt, wkb, wk2, pet, peb = _expand_cmp_weights(w_ck1[l], w_ck2[l], pe_cmp[l])
    wvt, wvb, wv2, _, _ = _expand_cmp_weights(w_cv1[l], w_cv2[l], pe_cmp[l])
    lbs = jnp.cumsum(jax.nn.softmax(hg_lb.astype(F32), axis=0), axis=0)[l].reshape(1, HG_HEADS * HG_DK)
    wr_pad = jnp.concatenate([w_router[l], jnp.zeros((d, LANES - N_EXPERTS), F32)], axis=1)
    wr3 = jnp.concatenate(_split3(wr_pad)[:2], axis=1)
    br_pad = jnp.concatenate([b_router[l], jnp.full((LANES - N_EXPERTS,), NEG, F32)]).reshape(1, LANES)

    cs_tab = _cos_sin_table(positions.reshape(n))
    nc_rows = seq // STRIDE_CMP
    cmp_end = jnp.minimum(jnp.arange(nc_rows) * STRIDE_CMP + L_CMP - 1, seq - 1)
    cs_cmp = _cos_sin_table(positions[:, cmp_end])

    qx, kcmp, vcmp, kva, gate, hg = _in_proj(x2, w_pad, cs_tab)
    kc, vc = _compress(kcmp.reshape(batch, nc_rows, STRIDE_CMP * LANES),
                       vcmp.reshape(batch, nc_rows, STRIDE_CMP * LANES),
                       pet, peb, wkt, wkb, wvt, wvb, wk2, wv2, cs_cmp)
    att = _nsa(qx, gate, kc, vc, kva, _overlap_matrix(nc_rows, seq), _block_mask_keys(seq),
               batch, seq)
    rec = _hgrn2(hg, lbs, hg_norm_g[l].reshape(1, HG_DV), batch, seq)
    h1t, ri, rw = _out_proj(att, rec, x2, w_o[l].astype(BF16), ln1_g[l].reshape(1, d),
                                ln1_b[l].reshape(1, d), wr3, br_pad)

    bm = MOE_BM
    a = n * TOP_K
    nblk = -(-(a + N_EXPERTS * (bm - 1)) // bm)
    rank, counts = _route(ri)
    pos, blk_expert, n_used, pad_tab = _route_tables(
        ri[:, :TOP_K], rank[:, :TOP_K], counts[0, :N_EXPERTS].astype(jnp.int32), nblk, bm)
    xs = _dispatch(pos, pad_tab, h1t, nblk * bm)
    ys = _experts(blk_expert, n_used, xs, w_gate_up[l], b_gate_up[l].reshape(N_EXPERTS, 1, 2 * D_FF),
                  w_down[l], b_down[l].reshape(N_EXPERTS, 1, d))
    out = _combine(pos, ys, h1t, rw,ln2_g[l].reshape(1, d), ln2_b[l].reshape(1, d))
    return out.reshape(batch, seq, d)
```
